```python
import math
import jax, jax.numpy as jnp
from jax import lax
import numpy as np

D_MODEL = 2048
BATCH = 16
SEQ = 256
DEPTH = 2
DEC_BATCH = 2
DEC_SEQ = 4096
PAST_LEN = 256

GRID_W = 64
N_MIXERS = 2
EPS = 1e-6
DN_HK = 16
DN_HV = 32
DN_DK = 128
DN_DV = 128
DN_CONV = 5
DN_CHUNK = 64
DN_QK_W = DN_HK * DN_DK
DN_V_W = DN_HV * DN_DV
DN_QKV = 2 * DN_QK_W + DN_V_W
DN_IN = DN_QKV + DN_V_W + 4 * DN_HV
NA_HEADS = 16
NA_HD = D_MODEL // NA_HEADS
WIN_R = 8
WIN_C = 16
CTX_QBLOCK = 128
PEER_HEADS = 8
PEER_NKEYS = 128
PEER_N = PEER_NKEYS * PEER_NKEYS
PEER_DQ = 256
PEER_TOPK = 16
PEER_TBLOCK = 128

kernel_name = 'hybrid_deltanet_natten_peer_diffusion_step'


def _rmsnorm(x, g):
    x32 = x.astype(jnp.float32)
    y = x32 * lax.rsqrt(jnp.mean(x32 * x32, axis=-1, keepdims=True) + EPS)
    return (y * g.astype(jnp.float32)).astype(x.dtype)


def _l2norm(x):
    x32 = x.astype(jnp.float32)
    return (x32 * lax.rsqrt(jnp.sum(x32 * x32, axis=-1, keepdims=True) + EPS)).astype(x.dtype)


def _modulate(x, shift, scale):
    return x * (1 + scale) + shift


def _adaln(cvec, w, b):
    m = jax.nn.silu(cvec) @ w + b
    if m.ndim == 2:
        m = m[:, None, :]
    return jnp.split(m, 6, axis=-1)


def _short_conv(x, w):
    ch = x.shape[-1]
    return lax.conv_general_dilated(
        x, w[:, None, :].astype(x.dtype), window_strides=(1,),
        padding=[(DN_CONV // 2, DN_CONV // 2)],
        dimension_numbers=('NWC', 'WIO', 'NWC'), feature_group_count=ch)


def _gated_delta_chunked(q, k, v, beta, g, s0):
    f32 = jnp.float32
    bsz, seq, nh, dk = q.shape
    dv = v.shape[-1]
    n = seq // DN_CHUNK

    def blk(t):
        return t.astype(f32).reshape(bsz, n, DN_CHUNK, nh, -1).transpose(1, 0, 3, 2, 4)

    def blk_s(t):
        return t.astype(f32).reshape(bsz, n, DN_CHUNK, nh).transpose(1, 0, 3, 2)

    q, k, v = blk(q), blk(k), blk(v)
    beta, g = blk_s(beta), blk_s(g)
    G = jnp.cumsum(g, axis=-1)
    idx = jnp.arange(DN_CHUNK)
    incl = idx[:, None] >= idx[None, :]
    strict = idx[:, None] > idx[None, :]
    decay = jnp.exp(jnp.where(incl, G[..., :, None] - G[..., None, :], -jnp.inf))
    kk = jnp.einsum('nbhid,nbhjd->nbhij', k, k)
    A = jnp.where(strict, beta[..., None] * decay * kk, 0.0)
    M = jnp.eye(DN_CHUNK, dtype=f32) + A
    rhs = jnp.concatenate([beta[..., None] * v, (beta * jnp.exp(G))[..., None] * k], axis=-1)
    sol = lax.linalg.triangular_solve(M, rhs, left_side=True, lower=True, unit_diagonal=True)
    u, wk = sol[..., :dv], sol[..., dv:]
    qk = decay * jnp.einsum('nbhid,nbhjd->nbhij', q, k)
    qg = q * jnp.exp(G)[..., None]
    kdec = k * jnp.exp(G[..., -1:] - G)[..., None]
    glast = jnp.exp(G[..., -1])

    def step(S, xs):
        u_c, wk_c, qk_c, qg_c, kd_c, gl_c = xs
        w = u_c - jnp.einsum('bhik,bhkv->bhiv', wk_c, S)
        o = jnp.einsum('bhik,bhkv->bhiv', qg_c, S) + jnp.einsum('bhij,bhjv->bhiv', qk_c, w)
        S = gl_c[..., None, None] * S + jnp.einsum('bhik,bhiv->bhkv', kd_c, w)
        return S, o

    S, o = lax.scan(step, s0.astype(f32), (u, wk, qk, qg, kdec, glast))
    o = o.transpose(1, 0, 3, 2, 4).reshape(bsz, seq, nh, dv)
    return o, S


def _deltanet(h, w_in, conv_w, a_log, dt_bias, norm_g, w_o, s0):
    f32 = jnp.float32
    bsz, seq, _ = h.shape
    proj = h @ w_in
    qkv = jax.nn.silu(_short_conv(proj[..., :DN_QKV], conv_w))
    z = proj[..., DN_QKV:DN_QKV + DN_V_W].reshape(bsz, seq, DN_HV, DN_DV).astype(f32)
    ba = proj[..., DN_QKV + DN_V_W:].astype(f32).reshape(bsz, seq, 2, 2, DN_HV)
    rep = DN_HV // DN_HK
    q = jnp.repeat(_l2norm(qkv[..., :DN_QK_W].reshape(bsz, seq, DN_HK, DN_DK)), rep, axis=2) * (DN_DK ** -0.5)
    k = jnp.repeat(_l2norm(qkv[..., DN_QK_W:2 * DN_QK_W].reshape(bsz, seq, DN_HK, DN_DK)), rep, axis=2)
    v = qkv[..., 2 * DN_QK_W:].reshape(bsz, seq, DN_HV, DN_DV)
    beta = jax.nn.sigmoid(ba[:, :, 0])
    g = -jnp.exp(a_log.astype(f32)) * jax.nn.softplus(ba[:, :, 1] + dt_bias.astype(f32))
    o_f, s_f = _gated_delta_chunked(q, k, v, beta[:, :, 0], g[:, :, 0], s0[:, 0])
    flip = lambda t: jnp.flip(t, axis=1)
    o_b, s_b = _gated_delta_chunked(flip(q), flip(k), flip(v), flip(beta[:, :, 1]), flip(g[:, :, 1]), s0[:, 1])
    o = o_f + flip(o_b)
    o = o * lax.rsqrt(jnp.mean(o * o, axis=-1, keepdims=True) + EPS) * norm_g.astype(f32) * jax.nn.silu(z)
    out = o.astype(h.dtype).reshape(bsz, seq, DN_V_W) @ w_o
    return out, jnp.stack([s_f, s_b], axis=1)


def _na_qkv(h, w_qkv):
    bsz, seq, _ = h.shape
    qkv = (h @ w_qkv).reshape(bsz, seq, 3, NA_HEADS, NA_HD)
    return qkv[:, :, 0] * (NA_HD ** -0.5), qkv[:, :, 1], qkv[:, :, 2]


def _na_context(h, w_qkv, w_o):
    bsz, seq, d = h.shape
    q, k, v = _na_qkv(h, w_qkv)
    nb = seq // CTX_QBLOCK
    qb = q.reshape(bsz, nb, CTX_QBLOCK, NA_HEADS, NA_HD).transpose(1, 0, 2, 3, 4)

    def blk(qq):
        s = jnp.einsum('bqhd,bchd->bhqc', qq, k).astype(jnp.float32)
        p = jax.nn.softmax(s, axis=-1).astype(v.dtype)
        return jnp.einsum('bhqc,bchd->bqhd', p, v)

    o = lax.map(blk, qb).transpose(1, 0, 2, 3, 4).reshape(bsz, seq, d)
    return o @ w_o, k, v


def _na_latent(h, w_qkv, rpb, w_o, k_ctx, v_ctx):
    f32 = jnp.float32
    bsz, seq, d = h.shape
    rows = seq // GRID_W
    wr = min(WIN_R, rows)
    n_lat = wr * GRID_W
    q, k, v = _na_qkv(h, w_qkv)
    qg = q.reshape(bsz, rows, GRID_W, NA_HEADS, NA_HD)
    kg = k.reshape(bsz, rows, GRID_W, NA_HEADS, NA_HD)
    vg = v.reshape(bsz, rows, GRID_W, NA_HEADS, NA_HD)
    col = jnp.arange(GRID_W)
    c0 = jnp.clip(col - WIN_C // 2, 0, GRID_W - WIN_C)
    col_ok = (col[None, :] >= c0[:, None]) & (col[None, :] < c0[:, None] + WIN_C)
    dc_idx = jnp.clip(col[None, :] - col[:, None], 1 - WIN_C, WIN_C - 1) + WIN_C - 1

    def row_step(r):
        r0 = jnp.clip(r - wr // 2, 0, rows - wr)
        q_r = lax.dynamic_index_in_dim(qg, r, axis=1, keepdims=False)
        k_w = lax.dynamic_slice_in_dim(kg, r0, wr, axis=1)
        v_w = lax.dynamic_slice_in_dim(vg, r0, wr, axis=1).reshape(bsz, n_lat, NA_HEADS, NA_HD)
        s_lat = jnp.einsum('bqhd,bwkhd->bhqwk', q_r, k_w).astype(f32)
        dr_idx = r0 + jnp.arange(wr) - r + WIN_R - 1
        bias = rpb[:, dr_idx[:, None, None], dc_idx[None, :, :]].transpose(0, 2, 1, 3).astype(f32)
        s_lat = jnp.where(col_ok[:, None, :], s_lat + bias, -jnp.inf).reshape(bsz, NA_HEADS, GRID_W, n_lat)
        s_ctx = jnp.einsum('bqhd,bchd->bhqc', q_r, k_ctx).astype(f32)
        p = jax.nn.softmax(jnp.concatenate([s_lat, s_ctx], axis=-1), axis=-1).astype(v_w.dtype)
        return (jnp.einsum('bhqn,bnhd->bqhd', p[..., :n_lat], v_w)
                + jnp.einsum('bhqc,bchd->bqhd', p[..., n_lat:], v_ctx))

    o = lax.map(row_step, jnp.arange(rows))
    o = o.transpose(1, 0, 2, 3, 4).reshape(bsz, seq, d)
    return o @ w_o


def _peer(h, w_q, keys, u_tab, v_tab):
    bsz, seq, d = h.shape
    t = bsz * seq
    x = h.reshape(t, d)
    q = (x @ w_q).reshape(t, PEER_HEADS, 2, PEER_DQ // 2)
    s = jnp.einsum('thpd,hpnd->thpn', q, keys).astype(jnp.float32)
    s1, i1 = lax.top_k(s[:, :, 0], PEER_TOPK)
    s2, i2 = lax.top_k(s[:, :, 1], PEER_TOPK)
    cand = (s1[..., :, None] + s2[..., None, :]).reshape(t, PEER_HEADS, PEER_TOPK * PEER_TOPK)
    best, pos = lax.top_k(cand, PEER_TOPK)
    eid = (jnp.take_along_axis(i1, pos // PEER_TOPK, axis=-1) * PEER_NKEYS
           + jnp.take_along_axis(i2, pos % PEER_TOPK, axis=-1))
    gate = jax.nn.softmax(best, axis=-1).astype(h.dtype)
    nsel = PEER_HEADS * PEER_TOPK
    nb = t // PEER_TBLOCK

    def blk(args):
        xb, eb, gb = args
        u = jnp.take(u_tab, eb, axis=0)
        a = jax.nn.gelu(jnp.einsum('tkd,td->tk', u, xb), approximate=False) * gb
        vv = jnp.take(v_tab, eb, axis=0)
        return jnp.einsum('tk,tkd->td', a, vv)

    out = lax.map(blk, (x.reshape(nb, PEER_TBLOCK, d), eid.reshape(nb, PEER_TBLOCK, nsel),
                        gate.reshape(nb, PEER_TBLOCK, nsel)))
    return out.reshape(bsz, seq, d)


def setup_inputs(seed: int = 0) -> dict:
    key = jax.random.key(seed)
    ks = jax.random.split(key, 32)
    f32 = jnp.float32

    def nrm(i, shape, s):
        return jax.random.normal(ks[i], shape, f32) * s

    D = D_MODEL
    n_dn = (DEPTH + 1) // 2
    n_na = DEPTH // 2
    dt = jnp.exp(jax.random.uniform(ks[20], (n_dn, 2, DN_HV), f32, minval=math.log(1e-3), maxval=math.log(1e-1)))
    return {
        'x_prompt': nrm(0, (BATCH, SEQ, D), 1.0),
        'x_sample': nrm(1, (DEC_BATCH, DEC_SEQ, D), 1.0),
        'c': nrm(2, (DEC_BATCH, D), 1.0),
        'state_delta': nrm(3, (DEC_BATCH, n_dn, 2, DN_HV, DN_DK, DN_DV), 0.5),
        'cache_k': nrm(4, (DEC_BATCH, n_na, PAST_LEN, NA_HEADS, NA_HD), 1.0),
        'cache_v': nrm(5, (DEC_BATCH, n_na, PAST_LEN, NA_HEADS, NA_HD), 1.0),
        'c_ctx': nrm(6, (D,), 1.0),
        'ada_w': nrm(7, (DEPTH, D, 6 * D), 0.5 * D ** -0.5),
        'ada_b': nrm(8, (DEPTH, 6 * D), 0.02),
        'norm1_g': 1.0 + nrm(9, (DEPTH, D), 0.02),
        'norm2_g': 1.0 + nrm(10, (DEPTH, D), 0.02),
        'final_g': 1.0 + nrm(11, (D,), 0.02),
        'dn_w_in': nrm(12, (n_dn, D, DN_IN), D ** -0.5),
        'dn_conv_w': nrm(13, (n_dn, DN_CONV, DN_QKV), DN_CONV ** -0.5),
        'dn_a_log': jnp.log(jax.random.uniform(ks[14], (n_dn, 2, DN_HV), f32, minval=1.0, maxval=16.0)),
        'dn_dt_bias': dt + jnp.log(-jnp.expm1(-dt)),
        'dn_norm_g': 1.0 + nrm(15, (n_dn, DN_DV), 0.02),
        'dn_w_o': nrm(16, (n_dn, DN_V_W, D), DN_V_W ** -0.5),
        'na_w_qkv': nrm(17, (n_na, D, 3 * D), D ** -0.5),
        'na_rpb': nrm(18, (n_na, NA_HEADS, 2 * WIN_R - 1, 2 * WIN_C - 1), 0.5),
        'na_w_o': nrm(19, (n_na, D, D), D ** -0.5),
        'peer_w_q': nrm(21, (DEPTH, D, PEER_HEADS * PEER_DQ), D ** -0.5),
        'peer_keys': nrm(22, (DEPTH, PEER_HEADS, 2, PEER_NKEYS, PEER_DQ // 2), (PEER_DQ // 2) ** -0.5),
        'peer_u': nrm(23, (DEPTH, PEER_N, D), D ** -0.5),
        'peer_v': nrm(24, (DEPTH, PEER_N, D), 0.5),
    }


def reference(x_prompt, x_sample, c, state_delta, cache_k, cache_v, c_ctx, ada_w, ada_b, norm1_g, norm2_g,
              final_g, dn_w_in, dn_conv_w, dn_a_log, dn_dt_bias, dn_norm_g, dn_w_o, na_w_qkv, na_rpb, na_w_o,
              peer_w_q, peer_keys, peer_u, peer_v):
    xp = x_prompt
    zero_state = jnp.zeros((x_prompt.shape[0], 2, DN_HV, DN_DK, DN_DV), jnp.float32)
    ctx_states, ctx_k, ctx_v = [], [], []
    for i in range(DEPTH):
        j = i // N_MIXERS
        sh1, sc1, g1, sh2, sc2, g2 = _adaln(c_ctx, ada_w[i], ada_b[i])
        h = _modulate(_rmsnorm(xp, norm1_g[i]), sh1, sc1)
        if i % N_MIXERS == 0:
            out, s_fin = _deltanet(h, dn_w_in[j], dn_conv_w[j], dn_a_log[j], dn_dt_bias[j], dn_norm_g[j],
                                   dn_w_o[j], zero_state)
            ctx_states.append(s_fin)
        else:
            out, k_c, v_c = _na_context(h, na_w_qkv[j], na_w_o[j])
            ctx_k.append(k_c)
            ctx_v.append(v_c)
        xp = xp + g1 * out
        h = _modulate(_rmsnorm(xp, norm2_g[i]), sh2, sc2)
        xp = xp + g2 * _peer(h, peer_w_q[i], peer_keys[i], peer_u[i], peer_v[i])
    y_prompt = _rmsnorm(xp, final_g)

    xs = x_sample
    for i in range(DEPTH):
        j = i // N_MIXERS
        sh1, sc1, g1, sh2, sc2, g2 = _adaln(c, ada_w[i], ada_b[i])
        h = _modulate(_rmsnorm(xs, norm1_g[i]), sh1, sc1)
        if i % N_MIXERS == 0:
            out, _ = _deltanet(h, dn_w_in[j], dn_conv_w[j], dn_a_log[j], dn_dt_bias[j], dn_norm_g[j],
                               dn_w_o[j], state_delta[:, j])
        else:
            out = _na_latent(h, na_w_qkv[j], na_rpb[j], na_w_o[j], cache_k[:, j], cache_v[:, j])
        xs = xs + g1 * out
        h = _modulate(_rmsnorm(xs, norm2_g[i]), sh2, sc2)
        xs = xs + g2 * _peer(h, peer_w_q[i], peer_keys[i], peer_u[i], peer_v[i])
    y_sample = _rmsnorm(xs, final_g)

    new_state_delta = jnp.stack(ctx_states, axis=1)
    new_cache_k = jnp.stack(ctx_k, axis=1)
    new_cache_v = jnp.stack(ctx_v, axis=1)
    return (y_prompt, y_sample, new_state_delta, new_cache_k, new_cache_v)
```

```python
import functools
import math

import jax
import jax.numpy as jnp
import numpy as np
from jax import lax
from jax.experimental import pallas as pl
from jax.experimental.pallas import tpu as pltpu

GRID_W = 64
WIN_R = 8
WIN_C = 16
DN_CHUNK = 64
DN_CONV = 5
PEER_TOPK = 16
EPS = 1e-6

LANES = 128
SUBLANES = 8
VMEM_LIMIT = 56 * 1024 * 1024

F32 = jnp.float32
BF16 = jnp.bfloat16
HI = lax.Precision.HIGHEST
NEG = -1e30


def _cparams(*sem):
    return pltpu.CompilerParams(dimension_semantics=sem, vmem_limit_bytes=VMEM_LIMIT)


def _dot(a, b, precision=None):
    return jnp.dot(a, b, preferred_element_type=F32, precision=precision)


def _dot_nt(a, b, precision=None):
    return lax.dot_general(a, b, (((1,), (1,)), ((), ())), preferred_element_type=F32, precision=precision)


def _silu(x):
    return x * (1.0 / (1.0 + jnp.exp(-x)))


def _pick(n, pref):
    b = min(n, pref)
    while n % b:
        b -= 1
    return b


def _adaln_kernel(c_ref, w_ref, b_ref, o_ref):
    o_ref[...] = _dot(_silu(c_ref[...]), w_ref[...], HI) + b_ref[...]


def _adaln(cvecs, ada_w, ada_b):
    depth, d, n6 = ada_w.shape
    r = cvecs.shape[0]
    tn = _pick(n6, 1024)
    return pl.pallas_call(
        _adaln_kernel,
        grid=(depth, n6 // tn),
        in_specs=[pl.BlockSpec((r, d), lambda i, j: (0, 0)),
                  pl.BlockSpec((None, d, tn), lambda i, j: (i, 0, j)),
                  pl.BlockSpec((None, 1, tn), lambda i, j: (i, 0, j))],
        out_specs=pl.BlockSpec((None, r, tn), lambda i, j: (i, 0, j)),
        out_shape=jax.ShapeDtypeStruct((depth, r, n6), F32),
        compiler_params=_cparams("parallel", "parallel"),
        name="adaln",
    )(cvecs, ada_w, ada_b.reshape(depth, 1, n6))


def _normmod_kernel(*refs, has_delta, has_mod, emit_x):
    it = iter(refs)
    x_ref = next(it)
    if has_delta:
        d_ref, gate_ref = next(it), next(it)
    g_ref = next(it)
    if has_mod:
        sh_ref, sc_ref = next(it), next(it)
    if emit_x:
        xo_ref = next(it)
    h_ref = next(it)
    x = x_ref[...]
    if has_delta:
        x = x + gate_ref[...] * d_ref[...]
    if emit_x:
        xo_ref[...] = x
    y = x * lax.rsqrt(jnp.mean(x * x, axis=-1, keepdims=True) + EPS) * g_ref[...]
    if has_mod:
        y = y * (1.0 + sc_ref[...]) + sh_ref[...]
    h_ref[...] = y.astype(h_ref.dtype)


def _normmod(x, gain, groups, delta=None, gate=None, shift=None, scale=None, out_dtype=BF16):
    t, d = x.shape
    tb = groups.block(256)
    has_delta, has_mod = delta is not None, shift is not None
    tok = pl.BlockSpec((tb, d), lambda i: (i, 0))
    grp = pl.BlockSpec((None, 1, d), lambda i: (groups.of_block(i, tb), 0, 0))
    args, specs = [x], [tok]
    if has_delta:
        args += [delta, gate]
        specs += [tok, grp]
    args.append(gain.reshape(1, d))
    specs.append(pl.BlockSpec((1, d), lambda i: (0, 0)))
    if has_mod:
        args += [shift, scale]
        specs += [grp, grp]
    out_shape, out_specs = [], []
    if has_delta:
        out_shape.append(jax.ShapeDtypeStruct((t, d), F32))
        out_specs.append(tok)
    out_shape.append(jax.ShapeDtypeStruct((t, d), out_dtype))
    out_specs.append(tok)
    res = pl.pallas_call(
        functools.partial(_normmod_kernel, has_delta=has_delta, has_mod=has_mod, emit_x=has_delta),
        grid=(t // tb,), in_specs=specs, out_specs=out_specs, out_shape=out_shape,
        compiler_params=_cparams("parallel"), name="normmod",
    )(*args)
    return (res[0], res[1]) if has_delta else (x, res[0])


class _Groups:
    def __init__(self, ctx_tokens, n_lat, lat_len):
        self.ctx_tokens, self.n_lat, self.lat_len = ctx_tokens, n_lat, lat_len
        self.total = ctx_tokens + n_lat * lat_len

    def block(self, pref):
        return _pick(math.gcd(self.ctx_tokens, self.lat_len), pref)

    def of_block(self, i, tb):
        nctx = self.ctx_tokens // tb
        return jnp.where(i < nctx, 0, 1 + (i - nctx) // (self.lat_len // tb))


def _matmul_kernel(*refs, has_res):
    if has_res:
        a_ref, w_ref, r_ref, g_ref, o_ref = refs
        o_ref[...] = r_ref[...] + g_ref[...] * _dot(a_ref[...], w_ref[...])
    else:
        a_ref, w_ref, o_ref = refs
        o_ref[...] = _dot(a_ref[...], w_ref[...]).astype(o_ref.dtype)


def _matmul(a, w, groups=None, res=None, gate=None, out_dtype=F32, n_cols=None, col0=0):
    m, k = a.shape
    n = w.shape[1] if n_cols is None else n_cols
    tm = _pick(m, 1024) if groups is None else groups.block(1024)
    tn = _pick(math.gcd(n, col0) if col0 else n, 512)
    cb0 = col0 // tn
    has_res = res is not None
    args = [a, w]
    specs = [pl.BlockSpec((tm, k), lambda i, j: (i, 0)), pl.BlockSpec((k, tn), lambda i, j: (0, cb0 + j))]
    if has_res:
        args += [res, gate]
        specs += [pl.BlockSpec((tm, tn), lambda i, j: (i, j)),
                  pl.BlockSpec((None, 1, tn), lambda i, j: (groups.of_block(i, tm), 0, j))]
    return pl.pallas_call(
        functools.partial(_matmul_kernel, has_res=has_res),
        grid=(m // tm, n // tn), in_specs=specs,
        out_specs=pl.BlockSpec((tm, tn), lambda i, j: (i, j)),
        out_shape=jax.ShapeDtypeStruct((m, n), out_dtype),
        compiler_params=_cparams("parallel", "parallel"), name="matmul",
    )(*args)


def _gates_kernel(h_ref, w_ref, alog_ref, dtb_ref, p_ref, *, hv, n_chunks):
    lane = lax.broadcasted_iota(jnp.int32, (DN_CHUNK, 4 * hv), 1)
    row = lax.broadcasted_iota(jnp.int32, (DN_CHUNK, DN_CHUNK), 0)
    col = lax.broadcasted_iota(jnp.int32, (DN_CHUNK, DN_CHUNK), 1)
    tri_lo = (row >= col).astype(F32)
    tri_up = (row <= col).astype(F32)
    bwd = ((lane // hv) % 2) == 1
    for c in range(n_chunks):
        sl = slice(c * DN_CHUNK, (c + 1) * DN_CHUNK)
        ba = _dot(h_ref[sl, :].astype(F32), w_ref[...], HI)
        beta = 1.0 / (1.0 + jnp.exp(-ba))
        z = ba + dtb_ref[...]
        softplus = jnp.maximum(z, 0.0) + jnp.log(1.0 + jnp.exp(-jnp.abs(z)))
        g = -jnp.exp(alog_ref[...]) * softplus
        cum = jnp.where(bwd, _dot(tri_up, g, HI), _dot(tri_lo, g, HI))
        p_ref[sl, :] = jnp.where(lane < 2 * hv, beta, cum)


def _gates(h, w_ba, a_log, dt_bias):
    t, d = h.shape
    hv = a_log.shape[-1]
    tb = _pick(t, 256)
    zeros = jnp.zeros((1, 2 * hv), F32)
    alog_row = jnp.concatenate([zeros, a_log.reshape(1, 2 * hv).astype(F32)], axis=1)
    dtb_row = jnp.concatenate([zeros, dt_bias.reshape(1, 2 * hv).astype(F32)], axis=1)
    return pl.pallas_call(
        functools.partial(_gates_kernel, hv=hv, n_chunks=tb // DN_CHUNK),
        grid=(t // tb,),
        in_specs=[pl.BlockSpec((tb, d), lambda i: (i, 0)),
                  pl.BlockSpec((d, 4 * hv), lambda i: (0, 0)),
                  pl.BlockSpec((1, 4 * hv), lambda i: (0, 0)),
                  pl.BlockSpec((1, 4 * hv), lambda i: (0, 0))],
        out_specs=pl.BlockSpec((tb, 4 * hv), lambda i: (i, 0)),
        out_shape=jax.ShapeDtypeStruct((t, 4 * hv), F32),
        compiler_params=_cparams("parallel"), name="dn_gates",
    )(h, w_ba, alog_row, dtb_row)


def _conv_kernel(hp_ref, hn_ref, prev_ref, cur_ref, next_ref, w_ref, o_ref, *, kind_blocks, q_scale):
    i, j = pl.program_id(0), pl.program_id(1)
    tb = cur_ref.shape[0]
    prev = prev_ref[...] * hp_ref[i].astype(F32)
    nxt = next_ref[...] * hn_ref[i].astype(F32)
    cur = cur_ref[...]
    ext = jnp.concatenate([prev, cur, nxt], axis=0)
    rows = ext.shape[0]
    half = DN_CONV // 2
    acc = cur * w_ref[half:half + 1, :]
    for tap in range(DN_CONV):
        if tap == half:
            continue
        shifted = pltpu.roll(ext, (half - tap) % rows, 0)[SUBLANES:SUBLANES + tb]
        acc = acc + shifted * w_ref[tap:tap + 1, :]
    y = _silu(acc)

    def l2(scale):
        for c in range(y.shape[1] // LANES):
            blk = y[:, c * LANES:(c + 1) * LANES]
            inv = lax.rsqrt(jnp.sum(blk * blk, axis=-1, keepdims=True) + EPS)
            o_ref[:, c * LANES:(c + 1) * LANES] = blk * (inv * scale)

    @pl.when(j < kind_blocks)
    def _():
        l2(q_scale)

    @pl.when(jnp.logical_and(j >= kind_blocks, j < 2 * kind_blocks))
    def _():
        l2(1.0)

    @pl.when(j >= 2 * kind_blocks)
    def _():
        o_ref[...] = y


def _short_conv(qkvz, conv_w, seq_lens, qk_width, dk):
    t = qkvz.shape[0]
    c = conv_w.shape[1]
    tb = seq_lens["tb"]
    cb = _pick(qk_width, 1024)
    nb8 = t // SUBLANES
    r8 = tb // SUBLANES
    grid_spec = pltpu.PrefetchScalarGridSpec(
        num_scalar_prefetch=2, grid=(t // tb, c // cb),
        in_specs=[pl.BlockSpec((SUBLANES, cb), lambda i, j, hp, hn: (jnp.maximum(i * r8 - 1, 0), j)),
                  pl.BlockSpec((tb, cb), lambda i, j, hp, hn: (i, j)),
                  pl.BlockSpec((SUBLANES, cb), lambda i, j, hp, hn: (jnp.minimum((i + 1) * r8, nb8 - 1), j)),
                  pl.BlockSpec((DN_CONV, cb), lambda i, j, hp, hn: (0, j))],
        out_specs=pl.BlockSpec((tb, cb), lambda i, j, hp, hn: (i, j)))
    return pl.pallas_call(
        functools.partial(_conv_kernel, kind_blocks=qk_width // cb, q_scale=dk ** -0.5),
        grid_spec=grid_spec,
        out_shape=jax.ShapeDtypeStruct((t, c), F32),
        compiler_params=_cparams("parallel", "parallel"), name="dn_conv",
    )(seq_lens["has_prev"], seq_lens["has_next"], qkvz, qkvz, qkvz, conv_w)


def _unit_tri_inverse(a):
    n = a.shape[0]
    eye = (lax.broadcasted_iota(jnp.int32, (n, n), 0) == lax.broadcasted_iota(jnp.int32, (n, n), 1)).astype(F32)
    x = -a
    t = eye + x
    p = 2
    while p < n:
        x = _dot(x, x, HI)
        t = t + _dot(t, x, HI)
        p *= 2
    return t


def _delta_kernel(*refs, hb, reverse, has_s0, emit_state):
    it = iter(refs)
    q_ref, k_ref, v_ref, pc_ref, pr_ref = (next(it) for _ in range(5))
    s0_ref = next(it) if has_s0 else None
    o_ref = next(it)
    sf_ref = next(it) if emit_state else None
    s_scr = next(it)
    n = pl.program_id(2)

    @pl.when(n == 0)
    def _():
        s_scr[...] = s0_ref[...] if has_s0 else jnp.zeros(s_scr.shape, F32)

    c = DN_CHUNK
    row = lax.broadcasted_iota(jnp.int32, (c, c), 0)
    col = lax.broadcasted_iota(jnp.int32, (c, c), 1)
    incl = (row <= col) if reverse else (row >= col)
    strict = (row < col) if reverse else (row > col)
    last = 0 if reverse else c - 1
    pc = pc_ref[...]
    pr = pr_ref[...]
    for kh in range(hb // 2):
        q = q_ref[:, kh * LANES:(kh + 1) * LANES]
        k = k_ref[:, kh * LANES:(kh + 1) * LANES]
        kk = _dot_nt(k, k, HI)
        qk = _dot_nt(q, k, HI)
        kt = k.T
        for r in range(2):
            hh = 2 * kh + r
            v = v_ref[:, hh * LANES:(hh + 1) * LANES]
            beta = pc[:, hh:hh + 1]
            g_col = pc[:, hb + hh:hb + hh + 1]
            g_row = pr[hb + hh:hb + hh + 1, :]
            g_last = g_row[:, last:last + 1]
            decay = jnp.where(incl, jnp.exp(jnp.where(incl, g_col - g_row, 0.0)), 0.0)
            a = jnp.where(strict, beta * decay * kk, 0.0)
            tinv = _unit_tri_inverse(a)
            e_g = jnp.exp(g_col)
            sol = _dot(tinv, jnp.concatenate([beta * v, (beta * e_g) * k], axis=1), HI)
            u, wk = sol[:, :LANES], sol[:, LANES:]
            s = s_scr[hh]
            both = _dot(jnp.concatenate([wk, q * e_g], axis=0), s, HI)
            w = u - both[:c]
            o_ref[:, hh * LANES:(hh + 1) * LANES] = both[c:] + _dot(decay * qk, w, HI)
            s_scr[hh] = jnp.exp(g_last) * s + _dot(kt * jnp.exp(g_last - g_row), w, HI)

    if emit_state:
        @pl.when(n == pl.num_programs(2) - 1)
        def _():
            sf_ref[...] = s_scr[...]


def _delta_scan(qkvc, p_col, p_row, s0, *, row0, n_seq, seq_len, hk, hv, dk, dv, direction, emit_state, hb=4):
    assert dk == LANES and dv == LANES and hv == 2 * hk
    hb = min(hb, hv)
    c = DN_CHUNK
    nch = seq_len // c
    hg = hv // hb
    rb0 = row0 // c
    reverse = direction == 1

    def chunk(b, n):
        return rb0 + b * nch + (nch - 1 - n if reverse else n)

    def ochunk(b, n):
        return b * nch + (nch - 1 - n if reverse else n)

    qw = (hb // 2) * dk
    vw = hb * dv
    in_specs = [pl.BlockSpec((c, qw), lambda b, g, n: (chunk(b, n), g)),
                pl.BlockSpec((c, qw), lambda b, g, n: (chunk(b, n), hk * dk // qw + g)),
                pl.BlockSpec((c, vw), lambda b, g, n: (chunk(b, n), 2 * hk * dk // vw + g)),
                pl.BlockSpec((None, None, c, 2 * hb), lambda b, g, n: (g, direction, chunk(b, n), 0)),
                pl.BlockSpec((None, None, None, 2 * hb, c), lambda b, g, n: (g, direction, chunk(b, n), 0, 0))]
    args = [qkvc, qkvc, qkvc, p_col, p_row]
    if s0 is not None:
        in_specs.append(pl.BlockSpec((None, hb, dk, dv), lambda b, g, n: (b, g, 0, 0)))
        args.append(s0)
    out_shape = [jax.ShapeDtypeStruct((n_seq * seq_len, hv * dv), F32)]
    out_specs = [pl.BlockSpec((c, vw), lambda b, g, n: (ochunk(b, n), g))]
    if emit_state:
        out_shape.append(jax.ShapeDtypeStruct((n_seq, hv, dk, dv), F32))
        out_specs.append(pl.BlockSpec((None, hb, dk, dv), lambda b, g, n: (b, g, 0, 0)))
    res = pl.pallas_call(
        functools.partial(_delta_kernel, hb=hb, reverse=reverse, has_s0=s0 is not None, emit_state=emit_state),
        grid=(n_seq, hg, nch), in_specs=in_specs, out_specs=out_specs, out_shape=out_shape,
        scratch_shapes=[pltpu.VMEM((hb, dk, dv), F32)],
        compiler_params=_cparams("parallel", "parallel", "arbitrary"), name="dn_scan",
    )(*args)
    return (res[0], res[1]) if emit_state else (res[0], None)


def _gated_norm_kernel(of_ref, ob_ref, z_ref, g_ref, o_ref):
    for c in range(o_ref.shape[1] // LANES):
        sl = slice(c * LANES, (c + 1) * LANES)
        o = of_ref[:, sl] + ob_ref[:, sl]
        o = o * lax.rsqrt(jnp.mean(o * o, axis=-1, keepdims=True) + EPS) * g_ref[...]
        o_ref[:, sl] = (o * _silu(z_ref[:, sl])).astype(o_ref.dtype)


def _gated_norm(o_f, o_b, qkvz, z_col0, norm_g):
    t, w = o_f.shape
    tb = _pick(t, 256)
    cb = _pick(math.gcd(w, z_col0), 1024)
    tok = pl.BlockSpec((tb, cb), lambda i, j: (i, j))
    return pl.pallas_call(
        _gated_norm_kernel, grid=(t // tb, w // cb),
        in_specs=[tok, tok, pl.BlockSpec((tb, cb), lambda i, j: (i, z_col0 // cb + j)),
                  pl.BlockSpec((1, LANES), lambda i, j: (0, 0))],
        out_specs=tok, out_shape=jax.ShapeDtypeStruct((t, w), BF16),
        compiler_params=_cparams("parallel", "parallel"), name="dn_gated_norm",
    )(o_f, o_b, qkvz, norm_g.reshape(1, LANES).astype(F32))


def _na_ctx_kernel(q_ref, k_ref, v_ref, o_ref, *, scale):
    q = (q_ref[...] * scale).astype(BF16)
    s = _dot_nt(q, k_ref[...].astype(BF16))
    p = jnp.exp(s - jnp.max(s, axis=-1, keepdims=True))
    o = _dot(p.astype(BF16), v_ref[...].astype(BF16)) / jnp.sum(p, axis=-1, keepdims=True)
    o_ref[...] = o.astype(o_ref.dtype)


def _na_context(qkv, n_seq, seq_len, heads):
    blk = lambda off: pl.BlockSpec((seq_len, LANES), lambda b, h: (b, off * heads + h))
    return pl.pallas_call(
        functools.partial(_na_ctx_kernel, scale=LANES ** -0.5),
        grid=(n_seq, heads), in_specs=[blk(0), blk(1), blk(2)],
        out_specs=pl.BlockSpec((seq_len, LANES), lambda b, h: (b, h)),
        out_shape=jax.ShapeDtypeStruct((n_seq * seq_len, heads * LANES), BF16),
        compiler_params=_cparams("parallel", "parallel"), name="na_ctx",
    )(qkv, qkv, qkv)


def _na_lat_kernel(q_ref, k_ref, v_ref, kc_ref, vc_ref, bias_ref, o_ref, kb_scr, vb_scr, *, rows, wr, scale):
    kb_scr[...] = k_ref[...].astype(BF16)
    vb_scr[...] = v_ref[...].astype(BF16)
    kc = kc_ref[...].astype(BF16)
    vc = vc_ref[...].astype(BF16)
    w = GRID_W

    def body(r, carry):
        r0 = jnp.clip(r - wr // 2, 0, rows - wr)
        q = (q_ref[pl.ds(pl.multiple_of(r * w, w), w), :] * scale).astype(BF16)
        win = pl.ds(pl.multiple_of(r0 * w, w), wr * w)
        s_lat = _dot_nt(q, kb_scr[win, :])
        d0 = r0 - r + WIN_R - 1
        s_lat = s_lat + jnp.concatenate([bias_ref[d0 + 2 * p] for p in range(wr // 2)], axis=1)
        s_ctx = _dot_nt(q, kc)
        m = jnp.maximum(jnp.max(s_lat, axis=-1, keepdims=True), jnp.max(s_ctx, axis=-1, keepdims=True))
        p_lat = jnp.exp(s_lat - m)
        p_ctx = jnp.exp(s_ctx - m)
        den = jnp.sum(p_lat, axis=-1, keepdims=True) + jnp.sum(p_ctx, axis=-1, keepdims=True)
        o = _dot(p_lat.astype(BF16), vb_scr[win, :]) + _dot(p_ctx.astype(BF16), vc)
        o_ref[pl.ds(pl.multiple_of(r * w, w), w), :] = (o / den).astype(o_ref.dtype)
        return carry

    lax.fori_loop(0, rows, body, 0)


def _na_bias_pairs(rpb):
    col = np.arange(GRID_W)
    c0 = np.clip(col - WIN_C // 2, 0, GRID_W - WIN_C)
    col_ok = (col[None, :] >= c0[:, None]) & (col[None, :] < c0[:, None] + WIN_C)
    dc_idx = np.clip(col[None, :] - col[:, None], 1 - WIN_C, WIN_C - 1) + WIN_C - 1
    bias = jnp.where(col_ok[None, None], rpb.astype(F32)[:, :, dc_idx], NEG)
    return jnp.concatenate([bias[:, :-1], bias[:, 1:]], axis=-1)


def _na_latent(qkv, cache_k, cache_v, rpb, row0, n_seq, seq_len, heads):
    rows = seq_len // GRID_W
    wr = min(WIN_R, rows)
    assert wr % 2 == 0 and row0 % seq_len == 0
    b0 = row0 // seq_len
    past = cache_k.shape[1]
    bias = _na_bias_pairs(rpb)
    blk = lambda off: pl.BlockSpec((seq_len, LANES), lambda b, h: (b0 + b, off * heads + h))
    cache = pl.BlockSpec((None, past, LANES), lambda b, h: (b, 0, h))
    return pl.pallas_call(
        functools.partial(_na_lat_kernel, rows=rows, wr=wr, scale=LANES ** -0.5),
        grid=(n_seq, heads),
        in_specs=[blk(0), blk(1), blk(2), cache, cache,
                  pl.BlockSpec((None,) + bias.shape[1:], lambda b, h: (h, 0, 0, 0))],
        out_specs=pl.BlockSpec((seq_len, LANES), lambda b, h: (b, h)),
        out_shape=jax.ShapeDtypeStruct((n_seq * seq_len, heads * LANES), BF16),
        scratch_shapes=[pltpu.VMEM((seq_len, LANES), BF16), pltpu.VMEM((seq_len, LANES), BF16)],
        compiler_params=_cparams("parallel", "parallel"), name="na_lat",
    )(qkv, qkv, qkv, cache_k, cache_v, bias)


def _top16(s, row):
    n = s.shape[0]
    vals = []
    rank = jnp.full(s.shape, PEER_TOPK, jnp.int32)
    for r in range(PEER_TOPK):
        m = jnp.max(s, axis=0, keepdims=True)
        idx = jnp.min(jnp.where(s == m, row, n), axis=0, keepdims=True)
        hit = row == idx
        vals.append(m)
        rank = jnp.where(hit, r, rank)
        s = jnp.where(hit, -jnp.inf, s)
    return jnp.concatenate(vals, axis=0), rank


def _peer_topk_kernel(q_ref, keys_ref, n_ref, e1_ref, b_ref, e2_ref, *, heads, half):
    tb = q_ref.shape[0]
    nk = keys_ref.shape[2]
    kk = PEER_TOPK
    row = lax.broadcasted_iota(jnp.int32, (nk, tb), 0)
    row_k = lax.broadcasted_iota(jnp.int32, (kk, tb), 0)
    flat = lax.broadcasted_iota(jnp.int32, (kk * kk, tb), 0)
    for h in range(heads):
        q1 = q_ref[:, (2 * h) * half:(2 * h + 1) * half]
        q2 = q_ref[:, (2 * h + 1) * half:(2 * h + 2) * half]
        s1 = _dot_nt(keys_ref[h, 0], q1, HI)
        s2 = _dot_nt(keys_ref[h, 1], q2, HI)
        v1, rank1 = _top16(s1, row)
        v2, rank2 = _top16(s2, row)
        cand = jnp.concatenate([v1[a:a + 1, :] + v2 for a in range(kk)], axis=0)
        cnt = jnp.zeros((kk, tb), jnp.int32)
        zsum = jnp.zeros((1, tb), F32)
        best0 = None
        for r in range(kk):
            m = jnp.max(cand, axis=0, keepdims=True)
            idx = jnp.min(jnp.where(cand == m, flat, kk * kk), axis=0, keepdims=True)
            best0 = m if r == 0 else best0
            zsum = zsum + jnp.exp(m - best0)
            cnt = cnt + (row_k == idx // kk).astype(jnp.int32)
            cand = jnp.where(flat == idx, -jnp.inf, cand)
        n_i = jnp.zeros((nk, tb), jnp.int32)
        for a in range(kk):
            n_i = jnp.where(rank1 == a, cnt[a:a + 1, :], n_i)
        n_ref[h] = n_i.astype(F32)
        e1_ref[h] = jnp.exp(s1 - v1[0:1, :]) / zsum
        b_ref[h] = rank2.astype(F32)
        e2_ref[h] = jnp.exp(s2 - v2[0:1, :])


def _peer_topk(q, keys):
    t = q.shape[0]
    heads, _, nk, half = keys.shape
    tb = _pick(t, 256)
    out = jax.ShapeDtypeStruct((heads, nk, t), F32)
    spec = pl.BlockSpec((heads, nk, tb), lambda i: (0, 0, i))
    return pl.pallas_call(
        functools.partial(_peer_topk_kernel, heads=heads, half=half),
        grid=(t // tb,),
        in_specs=[pl.BlockSpec((tb, q.shape[1]), lambda i: (i, 0)),
                  pl.BlockSpec(keys.shape, lambda i: (0, 0, 0, 0))],
        out_specs=[spec] * 4, out_shape=[out] * 4,
        compiler_params=_cparams("parallel"), name="peer_topk",
    )(q, keys)


def _gelu(x):
    return 0.5 * x * (1.0 + lax.erf(x * (2.0 ** -0.5)))


def _peer_dense_kernel(x_ref, u_ref, vt_ref, n_ref, e1_ref, b_ref, e2_ref, o_ref, acc_scr, w_scr, *, heads, ib):
    j = pl.program_id(1)

    @pl.when(j == 0)
    def _():
        acc_scr[...] = jnp.zeros(acc_scr.shape, F32)

    x = x_ref[...]
    nk = b_ref.shape[1]
    for ii in range(ib):
        a = _dot_nt(u_ref[ii * nk:(ii + 1) * nk, :], x)
        g = jnp.zeros(a.shape, F32)
        for h in range(heads):
            sel = b_ref[h] < n_ref[h, ii:ii + 1, :]
            g = g + jnp.where(sel, e1_ref[h, ii:ii + 1, :] * e2_ref[h], 0.0)
        w_scr[ii * nk:(ii + 1) * nk, :] = (g * _gelu(a)).astype(BF16)
    acc_scr[...] += _dot(vt_ref[...], w_scr[...])

    @pl.when(j == pl.num_programs(1) - 1)
    def _():
        o_ref[...] = acc_scr[...].T


def _peer_dense(h, u_tab, vt_tab, sel):
    t, d = h.shape
    ne = u_tab.shape[0]
    n_i, e1, b_j, e2 = sel
    heads, nk, _ = n_i.shape
    tb = _pick(t, 512)
    ib = 4
    eb = ib * nk
    n_i = n_i.reshape(heads, nk // ib, ib, t)
    e1 = e1.reshape(heads, nk // ib, ib, t)
    row_spec = pl.BlockSpec((heads, None, ib, tb), lambda i, j: (0, j, 0, i))
    col_spec = pl.BlockSpec((heads, nk, tb), lambda i, j: (0, 0, i))
    return pl.pallas_call(
        functools.partial(_peer_dense_kernel, heads=heads, ib=ib),
        grid=(t // tb, ne // eb),
        in_specs=[pl.BlockSpec((tb, d), lambda i, j: (i, 0)),
                  pl.BlockSpec((eb, d), lambda i, j: (j, 0)),
                  pl.BlockSpec((d, eb), lambda i, j: (0, j)),
                  row_spec, row_spec, col_spec, col_spec],
        out_specs=pl.BlockSpec((tb, d), lambda i, j: (i, 0)),
        out_shape=jax.ShapeDtypeStruct((t, d), F32),
        scratch_shapes=[pltpu.VMEM((d, tb), F32), pltpu.VMEM((eb, tb), BF16)],
        compiler_params=_cparams("parallel", "arbitrary"), name="peer_dense",
    )(h, u_tab, vt_tab, n_i, e1, b_j, e2)


def _peer(h, w_q, keys, u_tab, v_tab):
    q = _matmul(h, w_q.astype(BF16))
    sel = _peer_topk(q, keys.astype(F32))
    return _peer_dense(h, u_tab.astype(BF16), v_tab.T.astype(BF16), sel)


def _seq_flags(groups, n_ctx_seq, tb):
    ctx_len = groups.ctx_tokens // n_ctx_seq
    starts = np.concatenate([np.arange(n_ctx_seq) * ctx_len,
                             groups.ctx_tokens + np.arange(groups.n_lat) * groups.lat_len])
    ends = np.concatenate([starts[1:], [groups.total]])
    blk = np.arange(groups.total // tb) * tb
    has_prev = ~np.isin(blk, starts)
    has_next = ~np.isin(blk + tb, ends)
    return {"tb": tb, "has_prev": jnp.asarray(has_prev, jnp.int32), "has_next": jnp.asarray(has_next, jnp.int32)}


def _deltanet(h, x, gate, groups, n_ctx_seq, w_in, conv_w, a_log, dt_bias, norm_g, w_o, state_delta):
    hv = a_log.shape[-1]
    dk, dv = state_delta.shape[-2:]
    qkv_w = conv_w.shape[1]
    v_w = hv * dv
    hk = (qkv_w - v_w) // (2 * dk)
    ctx_len = groups.ctx_tokens // n_ctx_seq
    t = groups.total
    c = DN_CHUNK

    qkvz = _matmul(h, w_in.astype(BF16), n_cols=qkv_w + v_w)
    p = _gates(h, w_in[:, qkv_w + v_w:].astype(F32), a_log, dt_bias)
    tb = _pick(math.gcd(ctx_len, groups.lat_len), 256)
    qkvc = _short_conv(qkvz, conv_w.astype(F32), _seq_flags(groups, n_ctx_seq, tb), hk * dk, dk)

    hb = min(4, hv)
    hg = hv // hb
    p5 = p.reshape(t, 2, 2, hg, hb).transpose(3, 2, 0, 1, 4).reshape(hg, 2, t, 2 * hb)
    p_row = p5.reshape(hg, 2, t // c, c, 2 * hb).transpose(0, 1, 2, 4, 3)
    dims = dict(hk=hk, hv=hv, dk=dk, dv=dv, hb=hb)
    outs, states = [], []
    for direction in range(2):
        o_c, s_c = _delta_scan(qkvc, p5, p_row, None, row0=0, n_seq=n_ctx_seq, seq_len=ctx_len,
                               direction=direction, emit_state=True, **dims)
        o_l, _ = _delta_scan(qkvc, p5, p_row, state_delta[:, direction], row0=groups.ctx_tokens,
                             n_seq=groups.n_lat, seq_len=groups.lat_len, direction=direction,
                             emit_state=False, **dims)
        outs.append(jnp.concatenate([o_c, o_l], axis=0))
        states.append(s_c)
    og = _gated_norm(outs[0], outs[1], qkvz, qkv_w, norm_g)
    x = _matmul(og, w_o.astype(BF16), groups=groups, res=x, gate=gate)
    return x, jnp.stack(states, axis=1)


def _natten(h, x, gate, groups, n_ctx_seq, w_qkv, rpb, w_o, cache_k, cache_v):
    d = h.shape[1]
    heads = d // LANES
    ctx_len = groups.ctx_tokens // n_ctx_seq
    qkv = _matmul(h, w_qkv.astype(BF16))
    o_c = _na_context(qkv, n_ctx_seq, ctx_len, heads)
    n_lat, past = cache_k.shape[:2]
    o_l = _na_latent(qkv, cache_k.reshape(n_lat, past, d), cache_v.reshape(n_lat, past, d), rpb,
                     groups.ctx_tokens, groups.n_lat, groups.lat_len, heads)
    x = _matmul(jnp.concatenate([o_c, o_l], axis=0), w_o.astype(BF16), groups=groups, res=x, gate=gate)
    k_c = qkv[:groups.ctx_tokens, d:2 * d].reshape(n_ctx_seq, ctx_len, heads, LANES)
    v_c = qkv[:groups.ctx_tokens, 2 * d:].reshape(n_ctx_seq, ctx_len, heads, LANES)
    return x, k_c, v_c


def kernel(x_prompt, x_sample, c, state_delta, cache_k, cache_v, c_ctx, ada_w, ada_b, norm1_g, norm2_g, final_g,
           dn_w_in, dn_conv_w, dn_a_log, dn_dt_bias, dn_norm_g, dn_w_o, na_w_qkv, na_rpb, na_w_o,
           peer_w_q, peer_keys, peer_u, peer_v):
    n_ctx_seq, ctx_len, d = x_prompt.shape
    n_lat, lat_len, _ = x_sample.shape
    depth = ada_w.shape[0]
    groups = _Groups(n_ctx_seq * ctx_len, n_lat, lat_len)
    n_grp = 1 + n_lat

    x = jnp.concatenate([x_prompt.reshape(-1, d), x_sample.reshape(-1, d)], axis=0).astype(F32)
    pad = (-n_grp) % SUBLANES
    cvecs = jnp.concatenate([c_ctx[None], c, jnp.zeros((pad, d), c.dtype)], axis=0).astype(F32)
    mods = _adaln(cvecs, ada_w.astype(F32), ada_b.astype(F32))[:, :n_grp]
    mods = mods.reshape(depth, n_grp, 6, 1, d).transpose(0, 2, 1, 3, 4)

    states, ctx_k, ctx_v = [], [], []
    delta = gate = None
    for i in range(depth):
        sh1, sc1, g1, sh2, sc2, g2 = (mods[i, k] for k in range(6))
        j = i // 2
        x, h = _normmod(x, norm1_g[i].astype(F32), groups, delta, gate, sh1, sc1)
        if i % 2 == 0:
            x, s_fin = _deltanet(h, x, g1, groups, n_ctx_seq, dn_w_in[j], dn_conv_w[j], dn_a_log[j], dn_dt_bias[j],
                                 dn_norm_g[j], dn_w_o[j], state_delta[:, j])
            states.append(s_fin)
        else:
            x, k_c, v_c = _natten(h, x, g1, groups, n_ctx_seq, na_w_qkv[j], na_rpb[j], na_w_o[j],
                                  cache_k[:, j], cache_v[:, j])
            ctx_k.append(k_c)
            ctx_v.append(v_c)
        x, h = _normmod(x, norm2_g[i].astype(F32), groups, None, None, sh2, sc2)
        delta, gate = _peer(h, peer_w_q[i], peer_keys[i], peer_u[i], peer_v[i]), g2
    _, y = _normmod(x, final_g.astype(F32), groups, delta, gate, out_dtype=F32)
    y_prompt = y[:groups.ctx_tokens].reshape(x_prompt.shape)
    y_sample = y[groups.ctx_tokens:].reshape(x_sample.shape)
    return (y_prompt, y_sample, jnp.stack(states, axis=1), jnp.stack(ctx_k, axis=1), jnp.stack(ctx_v, axis=1))
```

```python
import functools
import math

import jax
import jax.numpy as jnp
import numpy as np
from jax import lax
from jax.experimental import pallas as pl
from jax.experimental.pallas import tpu as pltpu

GRID_W = 64
WIN_R = 8
WIN_C = 16
DN_CHUNK = 64
DN_CONV = 5
PEER_TOPK = 16
EPS = 1e-6
DN_HEADS_PER_STEP = 4
DN_CHUNKS_PER_STEP = 2

LANES = 128
SUBLANES = 8
BF16_ROWS = 16
VMEM_LIMIT = 56 * 1024 * 1024

F32 = jnp.float32
BF16 = jnp.bfloat16
HI = lax.Precision.HIGHEST
NEG = -1e30


def _cparams(*sem):
    return pltpu.CompilerParams(dimension_semantics=sem, vmem_limit_bytes=VMEM_LIMIT)


def _dot(a, b, precision=None):
    return jnp.dot(a, b, preferred_element_type=F32, precision=precision)


def _dot_nt(a, b, precision=None):
    return lax.dot_general(a, b, (((1,), (1,)), ((), ())), preferred_element_type=F32, precision=precision)


def _silu(x):
    return x * (1.0 / (1.0 + jnp.exp(-x)))


def _pick(n, pref):
    b = min(n, pref)
    while n % b:
        b -= 1
    return b


def _adaln_kernel(c_ref, w_ref, b_ref, o_ref):
    o_ref[...] = _dot(_silu(c_ref[...]), w_ref[...], HI) + b_ref[...]


def _adaln(cvecs, ada_w, ada_b):
    depth, d, n6 = ada_w.shape
    r = cvecs.shape[0]
    tn = _pick(n6, 1024)
    return pl.pallas_call(
        _adaln_kernel,
        grid=(depth, n6 // tn),
        in_specs=[pl.BlockSpec((r, d), lambda i, j: (0, 0)),
                  pl.BlockSpec((None, d, tn), lambda i, j: (i, 0, j)),
                  pl.BlockSpec((None, 1, tn), lambda i, j: (i, 0, j))],
        out_specs=pl.BlockSpec((None, r, tn), lambda i, j: (i, 0, j)),
        out_shape=jax.ShapeDtypeStruct((depth, r, n6), F32),
        compiler_params=_cparams("parallel", "parallel"),
        name="adaln",
    )(cvecs, ada_w, ada_b.reshape(depth, 1, n6))


def _normmod_kernel(*refs, has_delta, has_mod, emit_x):
    it = iter(refs)
    x_ref = next(it)
    if has_delta:
        d_ref, gate_ref = next(it), next(it)
    g_ref = next(it)
    if has_mod:
        sh_ref, sc_ref = next(it), next(it)
    if emit_x:
        xo_ref = next(it)
    h_ref = next(it)
    x = x_ref[...]
    if has_delta:
        x = x + gate_ref[...] * d_ref[...]
    if emit_x:
        xo_ref[...] = x
    y = x * lax.rsqrt(jnp.mean(x * x, axis=-1, keepdims=True) + EPS) * g_ref[...]
    if has_mod:
        y = y * (1.0 + sc_ref[...]) + sh_ref[...]
    h_ref[...] = y.astype(h_ref.dtype)


def _normmod(x, gain, groups, delta=None, gate=None, shift=None, scale=None, out_dtype=BF16):
    t, d = x.shape
    tb = groups.block(256)
    has_delta, has_mod = delta is not None, shift is not None
    tok = pl.BlockSpec((tb, d), lambda i: (i, 0))
    grp = pl.BlockSpec((None, 1, d), lambda i: (groups.of_block(i, tb), 0, 0))
    args, specs = [x], [tok]
    if has_delta:
        args += [delta, gate]
        specs += [tok, grp]
    args.append(gain.reshape(1, d))
    specs.append(pl.BlockSpec((1, d), lambda i: (0, 0)))
    if has_mod:
        args += [shift, scale]
        specs += [grp, grp]
    out_shape, out_specs = [], []
    if has_delta:
        out_shape.append(jax.ShapeDtypeStruct((t, d), F32))
        out_specs.append(tok)
    out_shape.append(jax.ShapeDtypeStruct((t, d), out_dtype))
    out_specs.append(tok)
    res = pl.pallas_call(
        functools.partial(_normmod_kernel, has_delta=has_delta, has_mod=has_mod, emit_x=has_delta),
        grid=(t // tb,), in_specs=specs, out_specs=out_specs, out_shape=out_shape,
        compiler_params=_cparams("parallel"), name="normmod",
    )(*args)
    return (res[0], res[1]) if has_delta else (x, res[0])


class _Groups:
    def __init__(self, ctx_tokens, n_lat, lat_len):
        self.ctx_tokens, self.n_lat, self.lat_len = ctx_tokens, n_lat, lat_len
        self.total = ctx_tokens + n_lat * lat_len

    def block(self, pref):
        return _pick(math.gcd(self.ctx_tokens, self.lat_len), pref)

    def of_block(self, i, tb):
        nctx = self.ctx_tokens // tb
        return jnp.where(i < nctx, 0, 1 + (i - nctx) // (self.lat_len // tb))


def _matmul_kernel(*refs, has_res):
    if has_res:
        a_ref, w_ref, r_ref, g_ref, o_ref = refs
        o_ref[...] = r_ref[...] + g_ref[...] * _dot(a_ref[...], w_ref[...])
    else:
        a_ref, w_ref, o_ref = refs
        o_ref[...] = _dot(a_ref[...], w_ref[...]).astype(o_ref.dtype)


def _matmul(a, w, groups=None, res=None, gate=None, out_dtype=F32, n_cols=None, col0=0):
    m, k = a.shape
    n = w.shape[1] if n_cols is None else n_cols
    tm = _pick(m, 1024) if groups is None else groups.block(1024)
    tn = _pick(math.gcd(n, col0) if col0 else n, 512)
    cb0 = col0 // tn
    has_res = res is not None
    args = [a, w]
    specs = [pl.BlockSpec((tm, k), lambda i, j: (i, 0)), pl.BlockSpec((k, tn), lambda i, j: (0, cb0 + j))]
    if has_res:
        args += [res, gate]
        specs += [pl.BlockSpec((tm, tn), lambda i, j: (i, j)),
                  pl.BlockSpec((None, 1, tn), lambda i, j: (groups.of_block(i, tm), 0, j))]
    return pl.pallas_call(
        functools.partial(_matmul_kernel, has_res=has_res),
        grid=(m // tm, n // tn), in_specs=specs,
        out_specs=pl.BlockSpec((tm, tn), lambda i, j: (i, j)),
        out_shape=jax.ShapeDtypeStruct((m, n), out_dtype),
        compiler_params=_cparams("parallel", "parallel"), name="matmul",
    )(*args)


def _gates_kernel(h_ref, w_ref, alog_ref, dtb_ref, p_ref, *, hv, n_chunks):
    lane = lax.broadcasted_iota(jnp.int32, (DN_CHUNK, 4 * hv), 1)
    row = lax.broadcasted_iota(jnp.int32, (DN_CHUNK, DN_CHUNK), 0)
    col = lax.broadcasted_iota(jnp.int32, (DN_CHUNK, DN_CHUNK), 1)
    tri_lo = (row >= col).astype(F32)
    tri_up = (row <= col).astype(F32)
    bwd = ((lane // hv) % 2) == 1
    for c in range(n_chunks):
        sl = slice(c * DN_CHUNK, (c + 1) * DN_CHUNK)
        ba = _dot(h_ref[sl, :].astype(F32), w_ref[...], HI)
        beta = 1.0 / (1.0 + jnp.exp(-ba))
        z = ba + dtb_ref[...]
        softplus = jnp.maximum(z, 0.0) + jnp.log(1.0 + jnp.exp(-jnp.abs(z)))
        g = -jnp.exp(alog_ref[...]) * softplus
        cum = jnp.where(bwd, _dot(tri_up, g, HI), _dot(tri_lo, g, HI))
        p_ref[sl, :] = jnp.where(lane < 2 * hv, beta, cum)


def _gates(h, w_ba, a_log, dt_bias):
    t, d = h.shape
    hv = a_log.shape[-1]
    tb = _pick(t, 256)
    zeros = jnp.zeros((1, 2 * hv), F32)
    alog_row = jnp.concatenate([zeros, a_log.reshape(1, 2 * hv).astype(F32)], axis=1)
    dtb_row = jnp.concatenate([zeros, dt_bias.reshape(1, 2 * hv).astype(F32)], axis=1)
    return pl.pallas_call(
        functools.partial(_gates_kernel, hv=hv, n_chunks=tb // DN_CHUNK),
        grid=(t // tb,),
        in_specs=[pl.BlockSpec((tb, d), lambda i: (i, 0)),
                  pl.BlockSpec((d, 4 * hv), lambda i: (0, 0)),
                  pl.BlockSpec((1, 4 * hv), lambda i: (0, 0)),
                  pl.BlockSpec((1, 4 * hv), lambda i: (0, 0))],
        out_specs=pl.BlockSpec((tb, 4 * hv), lambda i: (i, 0)),
        out_shape=jax.ShapeDtypeStruct((t, 4 * hv), F32),
        compiler_params=_cparams("parallel"), name="dn_gates",
    )(h, w_ba, alog_row, dtb_row)


def _conv_kernel(hp_ref, hn_ref, prev_ref, cur_ref, next_ref, w_ref, o_ref, *, kind_blocks, q_scale):
    i, j = pl.program_id(0), pl.program_id(1)
    tb = cur_ref.shape[0]
    prev = prev_ref[...] * hp_ref[i].astype(F32)
    nxt = next_ref[...] * hn_ref[i].astype(F32)
    cur = cur_ref[...]
    ext = jnp.concatenate([prev, cur, nxt], axis=0)
    rows = ext.shape[0]
    half = DN_CONV // 2
    acc = cur * w_ref[half:half + 1, :]
    for tap in range(DN_CONV):
        if tap == half:
            continue
        shifted = pltpu.roll(ext, (half - tap) % rows, 0)[SUBLANES:SUBLANES + tb]
        acc = acc + shifted * w_ref[tap:tap + 1, :]
    y = _silu(acc)

    def l2(scale):
        for c in range(y.shape[1] // LANES):
            blk = y[:, c * LANES:(c + 1) * LANES]
            inv = lax.rsqrt(jnp.sum(blk * blk, axis=-1, keepdims=True) + EPS)
            o_ref[:, c * LANES:(c + 1) * LANES] = blk * (inv * scale)

    @pl.when(j < kind_blocks)
    def _():
        l2(q_scale)

    @pl.when(jnp.logical_and(j >= kind_blocks, j < 2 * kind_blocks))
    def _():
        l2(1.0)

    @pl.when(j >= 2 * kind_blocks)
    def _():
        o_ref[...] = y


def _short_conv(qkvz, conv_w, seq_lens, qk_width, dk):
    t = qkvz.shape[0]
    c = conv_w.shape[1]
    tb = seq_lens["tb"]
    cb = _pick(qk_width, 1024)
    nb8 = t // SUBLANES
    r8 = tb // SUBLANES
    grid_spec = pltpu.PrefetchScalarGridSpec(
        num_scalar_prefetch=2, grid=(t // tb, c // cb),
        in_specs=[pl.BlockSpec((SUBLANES, cb), lambda i, j, hp, hn: (jnp.maximum(i * r8 - 1, 0), j)),
                  pl.BlockSpec((tb, cb), lambda i, j, hp, hn: (i, j)),
                  pl.BlockSpec((SUBLANES, cb), lambda i, j, hp, hn: (jnp.minimum((i + 1) * r8, nb8 - 1), j)),
                  pl.BlockSpec((DN_CONV, cb), lambda i, j, hp, hn: (0, j))],
        out_specs=pl.BlockSpec((tb, cb), lambda i, j, hp, hn: (i, j)))
    return pl.pallas_call(
        functools.partial(_conv_kernel, kind_blocks=qk_width // cb, q_scale=dk ** -0.5),
        grid_spec=grid_spec,
        out_shape=jax.ShapeDtypeStruct((t, c), F32),
        compiler_params=_cparams("parallel", "parallel"), name="dn_conv",
    )(seq_lens["has_prev"], seq_lens["has_next"], qkvz, qkvz, qkvz, conv_w)


def _bdot(a, b):
    return _dot(a.astype(BF16), b.astype(BF16))


def _bdot_nt(a, b):
    return _dot_nt(a.astype(BF16), b.astype(BF16))


def _split(a):
    hi = a.astype(BF16)
    return hi, (a - hi.astype(F32)).astype(BF16)


def _dot3(a, b):
    return _dot(a[0], b[0]) + (_dot(a[0], b[1]) + _dot(a[1], b[0]))


def _unit_tri_inverses(mats):
    n = mats[0].shape[0]
    eye = (lax.broadcasted_iota(jnp.int32, (n, n), 0) == lax.broadcasted_iota(jnp.int32, (n, n), 1)).astype(F32)
    ts = [eye - a for a in mats]
    xs = [_split(-a) for a in mats]
    p = 2
    while p < n:
        xs = [_split(_dot3(x, x)) for x in xs]
        ts = [t + _dot3(_split(t), x) for t, x in zip(ts, xs)]
        p *= 2
    return ts


def _delta_kernel(*refs, hb, cg, reverse, has_s0, emit_state):
    it = iter(refs)
    q_ref, k_ref, v_ref, pc_ref, pr_ref = (next(it) for _ in range(5))
    s0_ref = next(it) if has_s0 else None
    o_ref = next(it)
    sf_ref = next(it) if emit_state else None
    s_scr = next(it)
    n = pl.program_id(2)

    @pl.when(n == 0)
    def _():
        s_scr[...] = s0_ref[...] if has_s0 else jnp.zeros(s_scr.shape, F32)

    c = DN_CHUNK
    row = lax.broadcasted_iota(jnp.int32, (c, c), 0)
    col = lax.broadcasted_iota(jnp.int32, (c, c), 1)
    incl = (row <= col) if reverse else (row >= col)
    strict = (row < col) if reverse else (row > col)
    last = 0 if reverse else c - 1
    order = range(cg - 1, -1, -1) if reverse else range(cg)

    shared = {}
    for ci in order:
        rows = slice(ci * c, (ci + 1) * c)
        for kh in range(hb // 2):
            q = q_ref[rows, kh * LANES:(kh + 1) * LANES]
            k = k_ref[rows, kh * LANES:(kh + 1) * LANES]
            shared[ci, kh] = dict(q=q, k=k, kk=_bdot_nt(k, k), qk=_bdot_nt(q, k), kt=k.T)

    probs = [(ci, hh) for ci in order for hh in range(hb)]
    pre = {}
    for ci, hh in probs:
        pc = pc_ref[ci * c:(ci + 1) * c, :]
        pr = pr_ref[ci]
        beta = pc[:, hh:hh + 1]
        g_col = pc[:, hb + hh:hb + hh + 1]
        g_row = pr[hb + hh:hb + hh + 1, :]
        g_last = g_row[:, last:last + 1]
        decay = jnp.where(incl, jnp.exp(jnp.where(incl, g_col - g_row, 0.0)), 0.0)
        pre[ci, hh] = dict(beta=beta, e_g=jnp.exp(g_col), decay=decay, gl=jnp.exp(g_last),
                           kscale=jnp.exp(g_last - g_row),
                           a=jnp.where(strict, beta * decay * shared[ci, hh // 2]["kk"], 0.0))
    tinvs = _unit_tri_inverses([pre[p]["a"] for p in probs])

    prep = {}
    for (ci, hh), tinv in zip(probs, tinvs):
        sh, pp = shared[ci, hh // 2], pre[ci, hh]
        v = v_ref[ci * c:(ci + 1) * c, hh * LANES:(hh + 1) * LANES]
        sol = _bdot(tinv, jnp.concatenate([pp["beta"] * v, (pp["beta"] * pp["e_g"]) * sh["k"]], axis=1))
        prep[ci, hh] = dict(
            u=sol[:, :LANES],
            wk_qg=jnp.concatenate([sol[:, LANES:], sh["q"] * pp["e_g"]], axis=0).astype(BF16),
            aqk=(pp["decay"] * sh["qk"]).astype(BF16),
            kdt=(sh["kt"] * pp["kscale"]).astype(BF16),
            gl=pp["gl"])

    states = [s_scr[hh] for hh in range(hb)]
    for ci in order:
        boths = [_dot(prep[ci, hh]["wk_qg"], states[hh].astype(BF16)) for hh in range(hb)]
        ws = [(prep[ci, hh]["u"] - boths[hh][:c]).astype(BF16) for hh in range(hb)]
        for hh in range(hb):
            o_ref[ci * c:(ci + 1) * c, hh * LANES:(hh + 1) * LANES] = (
                boths[hh][c:] + _dot(prep[ci, hh]["aqk"], ws[hh]))
        states = [prep[ci, hh]["gl"] * states[hh] + _dot(prep[ci, hh]["kdt"], ws[hh]) for hh in range(hb)]
    for hh in range(hb):
        s_scr[hh] = states[hh]

    if emit_state:
        @pl.when(n == pl.num_programs(2) - 1)
        def _():
            sf_ref[...] = s_scr[...]


def _delta_scan(qkvc, p_col, p_row, s0, *, row0, n_seq, seq_len, hk, hv, dk, dv, direction, emit_state, hb, cg):
    assert dk == LANES and dv == LANES and hv == 2 * hk
    c = DN_CHUNK * cg
    nblk = seq_len // c
    hg = hv // hb
    rb0 = row0 // c
    reverse = direction == 1

    def blk(b, n):
        return rb0 + b * nblk + (nblk - 1 - n if reverse else n)

    def oblk(b, n):
        return b * nblk + (nblk - 1 - n if reverse else n)

    qw = (hb // 2) * dk
    vw = hb * dv
    in_specs = [pl.BlockSpec((c, qw), lambda b, g, n: (blk(b, n), g)),
                pl.BlockSpec((c, qw), lambda b, g, n: (blk(b, n), hk * dk // qw + g)),
                pl.BlockSpec((c, vw), lambda b, g, n: (blk(b, n), 2 * hk * dk // vw + g)),
                pl.BlockSpec((None, None, c, 2 * hb), lambda b, g, n: (g, direction, blk(b, n), 0)),
                pl.BlockSpec((None, None, cg, 2 * hb, DN_CHUNK), lambda b, g, n: (g, direction, blk(b, n), 0, 0))]
    args = [qkvc, qkvc, qkvc, p_col, p_row]
    if s0 is not None:
        in_specs.append(pl.BlockSpec((None, hb, dk, dv), lambda b, g, n: (b, g, 0, 0)))
        args.append(s0)
    out_shape = [jax.ShapeDtypeStruct((n_seq * seq_len, hv * dv), F32)]
    out_specs = [pl.BlockSpec((c, vw), lambda b, g, n: (oblk(b, n), g))]
    if emit_state:
        out_shape.append(jax.ShapeDtypeStruct((n_seq, hv, dk, dv), F32))
        out_specs.append(pl.BlockSpec((None, hb, dk, dv), lambda b, g, n: (b, g, 0, 0)))
    res = pl.pallas_call(
        functools.partial(_delta_kernel, hb=hb, cg=cg, reverse=reverse, has_s0=s0 is not None,
                          emit_state=emit_state),
        grid=(n_seq, hg, nblk), in_specs=in_specs, out_specs=out_specs, out_shape=out_shape,
        scratch_shapes=[pltpu.VMEM((hb, dk, dv), F32)],
        compiler_params=_cparams("parallel", "parallel", "arbitrary"), name="dn_scan",
    )(*args)
    return (res[0], res[1]) if emit_state else (res[0], None)


def _gated_norm_kernel(of_ref, ob_ref, z_ref, g_ref, o_ref):
    for c in range(o_ref.shape[1] // LANES):
        sl = slice(c * LANES, (c + 1) * LANES)
        o = of_ref[:, sl] + ob_ref[:, sl]
        o = o * lax.rsqrt(jnp.mean(o * o, axis=-1, keepdims=True) + EPS) * g_ref[...]
        o_ref[:, sl] = (o * _silu(z_ref[:, sl])).astype(o_ref.dtype)


def _gated_norm(o_f, o_b, qkvz, z_col0, norm_g):
    t, w = o_f.shape
    tb = _pick(t, 256)
    cb = _pick(math.gcd(w, z_col0), 1024)
    tok = pl.BlockSpec((tb, cb), lambda i, j: (i, j))
    return pl.pallas_call(
        _gated_norm_kernel, grid=(t // tb, w // cb),
        in_specs=[tok, tok, pl.BlockSpec((tb, cb), lambda i, j: (i, z_col0 // cb + j)),
                  pl.BlockSpec((1, LANES), lambda i, j: (0, 0))],
        out_specs=tok, out_shape=jax.ShapeDtypeStruct((t, w), BF16),
        compiler_params=_cparams("parallel", "parallel"), name="dn_gated_norm",
    )(o_f, o_b, qkvz, norm_g.reshape(1, LANES).astype(F32))


def _na_ctx_kernel(q_ref, k_ref, v_ref, o_ref, *, scale):
    q = (q_ref[...] * scale).astype(BF16)
    s = _dot_nt(q, k_ref[...].astype(BF16))
    p = jnp.exp(s - jnp.max(s, axis=-1, keepdims=True))
    o = _dot(p.astype(BF16), v_ref[...].astype(BF16)) / jnp.sum(p, axis=-1, keepdims=True)
    o_ref[...] = o.astype(o_ref.dtype)


def _na_context(qkv, n_seq, seq_len, heads):
    blk = lambda off: pl.BlockSpec((seq_len, LANES), lambda b, h: (b, off * heads + h))
    return pl.pallas_call(
        functools.partial(_na_ctx_kernel, scale=LANES ** -0.5),
        grid=(n_seq, heads), in_specs=[blk(0), blk(1), blk(2)],
        out_specs=pl.BlockSpec((seq_len, LANES), lambda b, h: (b, h)),
        out_shape=jax.ShapeDtypeStruct((n_seq * seq_len, heads * LANES), BF16),
        compiler_params=_cparams("parallel", "parallel"), name="na_ctx",
    )(qkv, qkv, qkv)


def _na_lat_kernel(q_ref, k_ref, v_ref, kc_ref, vc_ref, bias_ref, o_ref, kb_scr, vb_scr, *, rows, wr, scale):
    kb_scr[...] = k_ref[...].astype(BF16)
    vb_scr[...] = v_ref[...].astype(BF16)
    kc = kc_ref[...].astype(BF16)
    vc = vc_ref[...].astype(BF16)
    w = GRID_W

    def body(r, carry):
        r0 = jnp.clip(r - wr // 2, 0, rows - wr)
        q = (q_ref[pl.ds(pl.multiple_of(r * w, w), w), :] * scale).astype(BF16)
        win = pl.ds(pl.multiple_of(r0 * w, w), wr * w)
        s_lat = _dot_nt(q, kb_scr[win, :])
        d0 = r0 - r + WIN_R - 1
        s_lat = s_lat + jnp.concatenate([bias_ref[d0 + 2 * p] for p in range(wr // 2)], axis=1)
        s_ctx = _dot_nt(q, kc)
        m = jnp.maximum(jnp.max(s_lat, axis=-1, keepdims=True), jnp.max(s_ctx, axis=-1, keepdims=True))
        p_lat = jnp.exp(s_lat - m)
        p_ctx = jnp.exp(s_ctx - m)
        den = jnp.sum(p_lat, axis=-1, keepdims=True) + jnp.sum(p_ctx, axis=-1, keepdims=True)
        o = _dot(p_lat.astype(BF16), vb_scr[win, :]) + _dot(p_ctx.astype(BF16), vc)
        o_ref[pl.ds(pl.multiple_of(r * w, w), w), :] = (o / den).astype(o_ref.dtype)
        return carry

    lax.fori_loop(0, rows, body, 0)


def _na_bias_pairs(rpb):
    col = np.arange(GRID_W)
    c0 = np.clip(col - WIN_C // 2, 0, GRID_W - WIN_C)
    col_ok = (col[None, :] >= c0[:, None]) & (col[None, :] < c0[:, None] + WIN_C)
    dc_idx = np.clip(col[None, :] - col[:, None], 1 - WIN_C, WIN_C - 1) + WIN_C - 1
    bias = jnp.where(col_ok[None, None], rpb.astype(F32)[:, :, dc_idx], NEG)
    return jnp.concatenate([bias[:, :-1], bias[:, 1:]], axis=-1)


def _na_latent(qkv, cache_k, cache_v, rpb, row0, n_seq, seq_len, heads):
    rows = seq_len // GRID_W
    wr = min(WIN_R, rows)
    assert wr % 2 == 0 and row0 % seq_len == 0
    b0 = row0 // seq_len
    past = cache_k.shape[1]
    bias = _na_bias_pairs(rpb)
    blk = lambda off: pl.BlockSpec((seq_len, LANES), lambda b, h: (b0 + b, off * heads + h))
    cache = pl.BlockSpec((None, past, LANES), lambda b, h: (b, 0, h))
    return pl.pallas_call(
        functools.partial(_na_lat_kernel, rows=rows, wr=wr, scale=LANES ** -0.5),
        grid=(n_seq, heads),
        in_specs=[blk(0), blk(1), blk(2), cache, cache,
                  pl.BlockSpec((None,) + bias.shape[1:], lambda b, h: (h, 0, 0, 0))],
        out_specs=pl.BlockSpec((seq_len, LANES), lambda b, h: (b, h)),
        out_shape=jax.ShapeDtypeStruct((n_seq * seq_len, heads * LANES), BF16),
        scratch_shapes=[pltpu.VMEM((seq_len, LANES), BF16), pltpu.VMEM((seq_len, LANES), BF16)],
        compiler_params=_cparams("parallel", "parallel"), name="na_lat",
    )(qkv, qkv, qkv, cache_k, cache_v, bias)


def _top16(s, row):
    n = s.shape[0]
    vals = []
    rank = jnp.full(s.shape, PEER_TOPK, jnp.int32)
    for r in range(PEER_TOPK):
        m = jnp.max(s, axis=0, keepdims=True)
        idx = jnp.min(jnp.where(s == m, row, n), axis=0, keepdims=True)
        hit = row == idx
        vals.append(m)
        rank = jnp.where(hit, r, rank)
        s = jnp.where(hit, -jnp.inf, s)
    return jnp.concatenate(vals, axis=0), rank


def _peer_topk_kernel(q_ref, keys_ref, n_ref, e1_ref, b_ref, e2_ref, *, heads, half):
    tb = q_ref.shape[0]
    nk = keys_ref.shape[2]
    kk = PEER_TOPK
    row = lax.broadcasted_iota(jnp.int32, (nk, tb), 0)
    row_k = lax.broadcasted_iota(jnp.int32, (kk, tb), 0)
    flat = lax.broadcasted_iota(jnp.int32, (kk * kk, tb), 0)
    for h in range(heads):
        q1 = q_ref[:, (2 * h) * half:(2 * h + 1) * half]
        q2 = q_ref[:, (2 * h + 1) * half:(2 * h + 2) * half]
        s1 = _dot_nt(keys_ref[h, 0], q1, HI)
        s2 = _dot_nt(keys_ref[h, 1], q2, HI)
        v1, rank1 = _top16(s1, row)
        v2, rank2 = _top16(s2, row)
        cand = jnp.concatenate([v1[a:a + 1, :] + v2 for a in range(kk)], axis=0)
        cnt = jnp.zeros((kk, tb), jnp.int32)
        zsum = jnp.zeros((1, tb), F32)
        best0 = None
        for r in range(kk):
            m = jnp.max(cand, axis=0, keepdims=True)
            idx = jnp.min(jnp.where(cand == m, flat, kk * kk), axis=0, keepdims=True)
            best0 = m if r == 0 else best0
            zsum = zsum + jnp.exp(m - best0)
            cnt = cnt + (row_k == idx // kk).astype(jnp.int32)
            cand = jnp.where(flat == idx, -jnp.inf, cand)
        n_i = jnp.zeros((nk, tb), jnp.int32)
        for a in range(kk):
            n_i = jnp.where(rank1 == a, cnt[a:a + 1, :], n_i)
        n_ref[h] = n_i.astype(F32)
        e1_ref[h] = jnp.exp(s1 - v1[0:1, :]) / zsum
        b_ref[h] = rank2.astype(F32).astype(BF16)
        e2_ref[h] = jnp.exp(s2 - v2[0:1, :]).astype(BF16)


def _peer_topk(q, keys):
    t = q.shape[0]
    heads, _, nk, half = keys.shape
    tb = _pick(t, 256)
    out = [jax.ShapeDtypeStruct((heads, nk, t), dt) for dt in (F32, F32, BF16, BF16)]
    spec = pl.BlockSpec((heads, nk, tb), lambda i: (0, 0, i))
    return pl.pallas_call(
        functools.partial(_peer_topk_kernel, heads=heads, half=half),
        grid=(t // tb,),
        in_specs=[pl.BlockSpec((tb, q.shape[1]), lambda i: (i, 0)),
                  pl.BlockSpec(keys.shape, lambda i: (0, 0, 0, 0))],
        out_specs=[spec] * 4, out_shape=out,
        compiler_params=_cparams("parallel"), name="peer_topk",
    )(q, keys)


def _gelu(x):
    return 0.5 * x * (1.0 + lax.erf(x * (2.0 ** -0.5)))


def _peer_dense_kernel(x_ref, u_ref, vt_ref, n_ref, e1_ref, b_ref, e2_ref, o_ref, acc_scr, a_scr, w_scr, *,
                       heads, ib):
    j = pl.program_id(1)

    @pl.when(j == 0)
    def _():
        acc_scr[...] = jnp.zeros(acc_scr.shape, F32)

    nk = b_ref.shape[1]
    tb = x_ref.shape[0]
    a_scr[...] = _dot_nt(u_ref[...], x_ref[...])
    for ii in range(ib):
        rows = slice(ii * nk, (ii + 1) * nk)
        for tc in range(tb // LANES):
            lanes = slice(tc * LANES, (tc + 1) * LANES)
            g = jnp.zeros((nk // BF16_ROWS, BF16_ROWS, LANES), BF16)
            for h in range(heads):
                cnt = jnp.broadcast_to(n_ref[h, ii:ii + 1, lanes], (BF16_ROWS, LANES)).astype(BF16)[None]
                e1 = jnp.broadcast_to(e1_ref[h, ii:ii + 1, lanes], (BF16_ROWS, LANES)).astype(BF16)[None]
                b = b_ref[h, :, lanes].reshape(g.shape)
                e2 = e2_ref[h, :, lanes].reshape(g.shape)
                g = g + jnp.where(b < cnt, e1 * e2, jnp.zeros((), BF16))
            w_scr[rows, lanes] = g.reshape(nk, LANES) * _gelu(a_scr[rows, lanes]).astype(BF16)
    acc_scr[...] += _dot(vt_ref[...], w_scr[...])

    @pl.when(j == pl.num_programs(1) - 1)
    def _():
        o_ref[...] = acc_scr[...].T


def _peer_dense(h, u_tab, vt_tab, sel):
    t, d = h.shape
    ne = u_tab.shape[0]
    n_i, e1, b_j, e2 = sel
    heads, nk, _ = n_i.shape
    tb = _pick(t, 512)
    ib = SUBLANES
    eb = ib * nk
    n_i = n_i.reshape(heads, nk // ib, ib, t)
    e1 = e1.reshape(heads, nk // ib, ib, t)
    row_spec = pl.BlockSpec((heads, None, ib, tb), lambda i, j: (0, j, 0, i))
    col_spec = pl.BlockSpec((heads, nk, tb), lambda i, j: (0, 0, i))
    return pl.pallas_call(
        functools.partial(_peer_dense_kernel, heads=heads, ib=ib),
        grid=(t // tb, ne // eb),
        in_specs=[pl.BlockSpec((tb, d), lambda i, j: (i, 0)),
                  pl.BlockSpec((eb, d), lambda i, j: (j, 0)),
                  pl.BlockSpec((d, eb), lambda i, j: (0, j)),
                  row_spec, row_spec, col_spec, col_spec],
        out_specs=pl.BlockSpec((tb, d), lambda i, j: (i, 0)),
        out_shape=jax.ShapeDtypeStruct((t, d), F32),
        scratch_shapes=[pltpu.VMEM((d, tb), F32), pltpu.VMEM((eb, tb), F32), pltpu.VMEM((eb, tb), BF16)],
        compiler_params=_cparams("parallel", "arbitrary"), name="peer_dense",
    )(h, u_tab, vt_tab, n_i, e1, b_j, e2)


def _peer(h, w_q, keys, u_tab, v_tab):
    q = _matmul(h, w_q.astype(BF16))
    sel = _peer_topk(q, keys.astype(F32))
    return _peer_dense(h, u_tab.astype(BF16), v_tab.T.astype(BF16), sel)


def _seq_flags(groups, n_ctx_seq, tb):
    ctx_len = groups.ctx_tokens // n_ctx_seq
    starts = np.concatenate([np.arange(n_ctx_seq) * ctx_len,
                             groups.ctx_tokens + np.arange(groups.n_lat) * groups.lat_len])
    ends = np.concatenate([starts[1:], [groups.total]])
    blk = np.arange(groups.total // tb) * tb
    has_prev = ~np.isin(blk, starts)
    has_next = ~np.isin(blk + tb, ends)
    return {"tb": tb, "has_prev": jnp.asarray(has_prev, jnp.int32), "has_next": jnp.asarray(has_next, jnp.int32)}


def _deltanet(h, x, gate, groups, n_ctx_seq, w_in, conv_w, a_log, dt_bias, norm_g, w_o, state_delta):
    hv = a_log.shape[-1]
    dk, dv = state_delta.shape[-2:]
    qkv_w = conv_w.shape[1]
    v_w = hv * dv
    hk = (qkv_w - v_w) // (2 * dk)
    ctx_len = groups.ctx_tokens // n_ctx_seq
    t = groups.total
    c = DN_CHUNK

    qkvz = _matmul(h, w_in.astype(BF16), n_cols=qkv_w + v_w)
    p = _gates(h, w_in[:, qkv_w + v_w:].astype(F32), a_log, dt_bias)
    tb = _pick(math.gcd(ctx_len, groups.lat_len), 256)
    qkvc = _short_conv(qkvz, conv_w.astype(F32), _seq_flags(groups, n_ctx_seq, tb), hk * dk, dk)

    hb = min(DN_HEADS_PER_STEP, hv)
    cg = DN_CHUNKS_PER_STEP
    hg = hv // hb
    p5 = p.reshape(t, 2, 2, hg, hb).transpose(3, 2, 0, 1, 4).reshape(hg, 2, t, 2 * hb)
    p_row = p5.reshape(hg, 2, t // c, c, 2 * hb).transpose(0, 1, 2, 4, 3)
    dims = dict(hk=hk, hv=hv, dk=dk, dv=dv, hb=hb, cg=cg)
    outs, states = [], []
    for direction in range(2):
        o_c, s_c = _delta_scan(qkvc, p5, p_row, None, row0=0, n_seq=n_ctx_seq, seq_len=ctx_len,
                               direction=direction, emit_state=True, **dims)
        o_l, _ = _delta_scan(qkvc, p5, p_row, state_delta[:, direction], row0=groups.ctx_tokens,
                             n_seq=groups.n_lat, seq_len=groups.lat_len, direction=direction,
                             emit_state=False, **dims)
        outs.append(jnp.concatenate([o_c, o_l], axis=0))
        states.append(s_c)
    og = _gated_norm(outs[0], outs[1], qkvz, qkv_w, norm_g)
    x = _matmul(og, w_o.astype(BF16), groups=groups, res=x, gate=gate)
    return x, jnp.stack(states, axis=1)


def _natten(h, x, gate, groups, n_ctx_seq, w_qkv, rpb, w_o, cache_k, cache_v):
    d = h.shape[1]
    heads = d // LANES
    ctx_len = groups.ctx_tokens // n_ctx_seq
    qkv = _matmul(h, w_qkv.astype(BF16))
    o_c = _na_context(qkv, n_ctx_seq, ctx_len, heads)
    n_lat, past = cache_k.shape[:2]
    o_l = _na_latent(qkv, cache_k.reshape(n_lat, past, d), cache_v.reshape(n_lat, past, d), rpb,
                     groups.ctx_tokens, groups.n_lat, groups.lat_len, heads)
    x = _matmul(jnp.concatenate([o_c, o_l], axis=0), w_o.astype(BF16), groups=groups, res=x, gate=gate)
    k_c = qkv[:groups.ctx_tokens, d:2 * d].reshape(n_ctx_seq, ctx_len, heads, LANES)
    v_c = qkv[:groups.ctx_tokens, 2 * d:].reshape(n_ctx_seq, ctx_len, heads, LANES)
    return x, k_c, v_c


def kernel(x_prompt, x_sample, c, state_delta, cache_k, cache_v, c_ctx, ada_w, ada_b, norm1_g, norm2_g, final_g,
           dn_w_in, dn_conv_w, dn_a_log, dn_dt_bias, dn_norm_g, dn_w_o, na_w_qkv, na_rpb, na_w_o,
           peer_w_q, peer_keys, peer_u, peer_v):
    n_ctx_seq, ctx_len, d = x_prompt.shape
    n_lat, lat_len, _ = x_sample.shape
    depth = ada_w.shape[0]
    groups = _Groups(n_ctx_seq * ctx_len, n_lat, lat_len)
    n_grp = 1 + n_lat

    x = jnp.concatenate([x_prompt.reshape(-1, d), x_sample.reshape(-1, d)], axis=0).astype(F32)
    pad = (-n_grp) % SUBLANES
    cvecs = jnp.concatenate([c_ctx[None], c, jnp.zeros((pad, d), c.dtype)], axis=0).astype(F32)
    mods = _adaln(cvecs, ada_w.astype(F32), ada_b.astype(F32))[:, :n_grp]
    mods = mods.reshape(depth, n_grp, 6, 1, d).transpose(0, 2, 1, 3, 4)

    states, ctx_k, ctx_v = [], [], []
    delta = gate = None
    for i in range(depth):
        sh1, sc1, g1, sh2, sc2, g2 = (mods[i, k] for k in range(6))
        j = i // 2
        x, h = _normmod(x, norm1_g[i].astype(F32), groups, delta, gate, sh1, sc1)
        if i % 2 == 0:
            x, s_fin = _deltanet(h, x, g1, groups, n_ctx_seq, dn_w_in[j], dn_conv_w[j], dn_a_log[j], dn_dt_bias[j],
                                 dn_norm_g[j], dn_w_o[j], state_delta[:, j])
            states.append(s_fin)
        else:
            x, k_c, v_c = _natten(h, x, g1, groups, n_ctx_seq, na_w_qkv[j], na_rpb[j], na_w_o[j],
                                  cache_k[:, j], cache_v[:, j])
            ctx_k.append(k_c)
            ctx_v.append(v_c)
        x, h = _normmod(x, norm2_g[i].astype(F32), groups, None, None, sh2, sc2)
        delta, gate = _peer(h, peer_w_q[i], peer_keys[i], peer_u[i], peer_v[i]), g2
    _, y = _normmod(x, final_g.astype(F32), groups, delta, gate, out_dtype=F32)
    y_prompt = y[:groups.ctx_tokens].reshape(x_prompt.shape)
    y_sample = y[groups.ctx_tokens:].reshape(x_sample.shape)
    return (y_prompt, y_sample, jnp.stack(states, axis=1), jnp.stack(ctx_k, axis=1), jnp.stack(ctx_v, axis=1))
```

```python
import functools
import math

import jax
import jax.numpy as jnp
import numpy as np
from jax import lax
from jax.experimental import pallas as pl
from jax.experimental.pallas import tpu as pltpu

GRID_W = 64
WIN_R = 8
WIN_C = 16
DN_CHUNK = 64
DN_CONV = 5
PEER_TOPK = 16
EPS = 1e-6
DN_HEADS_PER_STEP = 8
DN_CHUNKS_PER_STEP = 2

LANES = 128
SUBLANES = 8
BF16_ROWS = 16
MXU_WIDTH = 256
VMEM_LIMIT = 56 * 1024 * 1024

F32 = jnp.float32
BF16 = jnp.bfloat16
HI = lax.Precision.HIGHEST
NEG = -1e30


def _cparams(*sem):
    return pltpu.CompilerParams(dimension_semantics=sem, vmem_limit_bytes=VMEM_LIMIT)


def _dot(a, b, precision=None):
    return jnp.dot(a, b, preferred_element_type=F32, precision=precision)


def _dot_nt(a, b, precision=None):
    return lax.dot_general(a, b, (((1,), (1,)), ((), ())), preferred_element_type=F32, precision=precision)


def _silu(x):
    return x * (1.0 / (1.0 + jnp.exp(-x)))


def _pick(n, pref):
    b = min(n, pref)
    while n % b:
        b -= 1
    return b


def _adaln_kernel(c_ref, w_ref, b_ref, o_ref):
    o_ref[...] = _dot(_silu(c_ref[...]), w_ref[...], HI) + b_ref[...]


def _adaln(cvecs, ada_w, ada_b):
    depth, d, n6 = ada_w.shape
    r = cvecs.shape[0]
    tn = _pick(n6, 1024)
    return pl.pallas_call(
        _adaln_kernel,
        grid=(depth, n6 // tn),
        in_specs=[pl.BlockSpec((r, d), lambda i, j: (0, 0)),
                  pl.BlockSpec((None, d, tn), lambda i, j: (i, 0, j)),
                  pl.BlockSpec((None, 1, tn), lambda i, j: (i, 0, j))],
        out_specs=pl.BlockSpec((None, r, tn), lambda i, j: (i, 0, j)),
        out_shape=jax.ShapeDtypeStruct((depth, r, n6), F32),
        compiler_params=_cparams("parallel", "parallel"),
        name="adaln",
    )(cvecs, ada_w, ada_b.reshape(depth, 1, n6))


def _normmod_kernel(*refs, has_delta, has_mod, emit_x, emit_t):
    it = iter(refs)
    x_ref = next(it)
    if has_delta:
        d_ref, gate_ref = next(it), next(it)
    g_ref = next(it)
    if has_mod:
        sh_ref, sc_ref = next(it), next(it)
    if emit_x:
        xo_ref = next(it)
    h_ref = next(it)
    x = x_ref[...]
    if has_delta:
        x = x + gate_ref[...] * d_ref[...]
    if emit_x:
        xo_ref[...] = x
    y = x * lax.rsqrt(jnp.mean(x * x, axis=-1, keepdims=True) + EPS) * g_ref[...]
    if has_mod:
        y = y * (1.0 + sc_ref[...]) + sh_ref[...]
    h_ref[...] = y.astype(h_ref.dtype)
    if emit_t:
        next(it)[...] = y.T.astype(h_ref.dtype)


def _normmod(x, gain, groups, delta=None, gate=None, shift=None, scale=None, out_dtype=BF16, transposed=False,
             rows=None):
    t, d = x.shape
    tb = groups.block(256)
    row0, t_out = (0, t) if rows is None else rows
    b0 = row0 // tb
    has_delta, has_mod = delta is not None, shift is not None
    emit_x = has_delta and rows is None
    tok = pl.BlockSpec((tb, d), lambda i: (b0 + i, 0))
    out_tok = pl.BlockSpec((tb, d), lambda i: (i, 0))
    grp = pl.BlockSpec((None, 1, d), lambda i: (groups.of_block(b0 + i, tb), 0, 0))
    args, specs = [x], [tok]
    if has_delta:
        args += [delta, gate]
        specs += [tok, grp]
    args.append(gain.reshape(1, d))
    specs.append(pl.BlockSpec((1, d), lambda i: (0, 0)))
    if has_mod:
        args += [shift, scale]
        specs += [grp, grp]
    out_shape, out_specs = [], []
    if emit_x:
        out_shape.append(jax.ShapeDtypeStruct((t, d), F32))
        out_specs.append(out_tok)
    out_shape.append(jax.ShapeDtypeStruct((t_out, d), out_dtype))
    out_specs.append(out_tok)
    if transposed:
        out_shape.append(jax.ShapeDtypeStruct((d, t_out), out_dtype))
        out_specs.append(pl.BlockSpec((d, tb), lambda i: (0, i)))
    res = pl.pallas_call(
        functools.partial(_normmod_kernel, has_delta=has_delta, has_mod=has_mod, emit_x=emit_x,
                          emit_t=transposed),
        grid=(t_out // tb,), in_specs=specs, out_specs=out_specs, out_shape=out_shape,
        compiler_params=_cparams("parallel"), name="normmod",
    )(*args)
    if rows is not None:
        return res[0]
    res = list(res) if has_delta else [x] + list(res)
    return tuple(res)


class _Groups:
    def __init__(self, ctx_tokens, n_lat, lat_len):
        self.ctx_tokens, self.n_lat, self.lat_len = ctx_tokens, n_lat, lat_len
        self.total = ctx_tokens + n_lat * lat_len

    def block(self, pref):
        return _pick(math.gcd(self.ctx_tokens, self.lat_len), pref)

    def of_block(self, i, tb):
        nctx = self.ctx_tokens // tb
        return jnp.where(i < nctx, 0, 1 + (i - nctx) // (self.lat_len // tb))


def _matmul_kernel(*refs, has_res):
    if has_res:
        a_ref, w_ref, r_ref, g_ref, o_ref = refs
        o_ref[...] = r_ref[...] + g_ref[...] * _dot(a_ref[...], w_ref[...])
    else:
        a_ref, w_ref, o_ref = refs
        o_ref[...] = _dot(a_ref[...], w_ref[...]).astype(o_ref.dtype)


def _matmul(a, w, groups=None, res=None, gate=None, out_dtype=F32, n_cols=None, col0=0):
    m, k = a.shape
    n = w.shape[1] if n_cols is None else n_cols
    tm = _pick(m, 1024) if groups is None else groups.block(1024)
    tn = _pick(math.gcd(n, col0) if col0 else n, 512)
    cb0 = col0 // tn
    has_res = res is not None
    args = [a, w]
    specs = [pl.BlockSpec((tm, k), lambda i, j: (i, 0)), pl.BlockSpec((k, tn), lambda i, j: (0, cb0 + j))]
    if has_res:
        args += [res, gate]
        specs += [pl.BlockSpec((tm, tn), lambda i, j: (i, j)),
                  pl.BlockSpec((None, 1, tn), lambda i, j: (groups.of_block(i, tm), 0, j))]
    return pl.pallas_call(
        functools.partial(_matmul_kernel, has_res=has_res),
        grid=(m // tm, n // tn), in_specs=specs,
        out_specs=pl.BlockSpec((tm, tn), lambda i, j: (i, j)),
        out_shape=jax.ShapeDtypeStruct((m, n), out_dtype),
        compiler_params=_cparams("parallel", "parallel"), name="matmul",
    )(*args)


def _gates_kernel(h_ref, w_ref, alog_ref, dtb_ref, p_ref, *, hv, n_chunks):
    lane = lax.broadcasted_iota(jnp.int32, (DN_CHUNK, 4 * hv), 1)
    row = lax.broadcasted_iota(jnp.int32, (DN_CHUNK, DN_CHUNK), 0)
    col = lax.broadcasted_iota(jnp.int32, (DN_CHUNK, DN_CHUNK), 1)
    tri_lo = (row >= col).astype(F32)
    tri_up = (row <= col).astype(F32)
    bwd = ((lane // hv) % 2) == 1
    for c in range(n_chunks):
        sl = slice(c * DN_CHUNK, (c + 1) * DN_CHUNK)
        ba = _dot(h_ref[sl, :].astype(F32), w_ref[...], HI)
        beta = 1.0 / (1.0 + jnp.exp(-ba))
        z = ba + dtb_ref[...]
        softplus = jnp.maximum(z, 0.0) + jnp.log(1.0 + jnp.exp(-jnp.abs(z)))
        g = -jnp.exp(alog_ref[...]) * softplus
        cum = jnp.where(bwd, _dot(tri_up, g, HI), _dot(tri_lo, g, HI))
        p_ref[sl, :] = jnp.where(lane < 2 * hv, beta, cum)


def _gates(h, w_ba, a_log, dt_bias):
    t, d = h.shape
    hv = a_log.shape[-1]
    tb = _pick(t, 256)
    zeros = jnp.zeros((1, 2 * hv), F32)
    alog_row = jnp.concatenate([zeros, a_log.reshape(1, 2 * hv).astype(F32)], axis=1)
    dtb_row = jnp.concatenate([zeros, dt_bias.reshape(1, 2 * hv).astype(F32)], axis=1)
    return pl.pallas_call(
        functools.partial(_gates_kernel, hv=hv, n_chunks=tb // DN_CHUNK),
        grid=(t // tb,),
        in_specs=[pl.BlockSpec((tb, d), lambda i: (i, 0)),
                  pl.BlockSpec((d, 4 * hv), lambda i: (0, 0)),
                  pl.BlockSpec((1, 4 * hv), lambda i: (0, 0)),
                  pl.BlockSpec((1, 4 * hv), lambda i: (0, 0))],
        out_specs=pl.BlockSpec((tb, 4 * hv), lambda i: (i, 0)),
        out_shape=jax.ShapeDtypeStruct((t, 4 * hv), F32),
        compiler_params=_cparams("parallel"), name="dn_gates",
    )(h, w_ba, alog_row, dtb_row)


def _conv_kernel(hp_ref, hn_ref, prev_ref, cur_ref, next_ref, w_ref, o_ref, *, kind_blocks, q_scale):
    i, j = pl.program_id(0), pl.program_id(1)
    tb = cur_ref.shape[0]
    prev = prev_ref[...] * hp_ref[i].astype(F32)
    nxt = next_ref[...] * hn_ref[i].astype(F32)
    cur = cur_ref[...]
    ext = jnp.concatenate([prev, cur, nxt], axis=0)
    rows = ext.shape[0]
    half = DN_CONV // 2
    acc = cur * w_ref[half:half + 1, :]
    for tap in range(DN_CONV):
        if tap == half:
            continue
        shifted = pltpu.roll(ext, (half - tap) % rows, 0)[SUBLANES:SUBLANES + tb]
        acc = acc + shifted * w_ref[tap:tap + 1, :]
    y = _silu(acc)

    def l2(scale):
        for c in range(y.shape[1] // LANES):
            blk = y[:, c * LANES:(c + 1) * LANES]
            inv = lax.rsqrt(jnp.sum(blk * blk, axis=-1, keepdims=True) + EPS)
            o_ref[:, c * LANES:(c + 1) * LANES] = blk * (inv * scale)

    @pl.when(j < kind_blocks)
    def _():
        l2(q_scale)

    @pl.when(jnp.logical_and(j >= kind_blocks, j < 2 * kind_blocks))
    def _():
        l2(1.0)

    @pl.when(j >= 2 * kind_blocks)
    def _():
        o_ref[...] = y


def _short_conv(qkvz, conv_w, seq_lens, qk_width, dk):
    t = qkvz.shape[0]
    c = conv_w.shape[1]
    tb = seq_lens["tb"]
    cb = _pick(qk_width, 1024)
    nb8 = t // SUBLANES
    r8 = tb // SUBLANES
    grid_spec = pltpu.PrefetchScalarGridSpec(
        num_scalar_prefetch=2, grid=(t // tb, c // cb),
        in_specs=[pl.BlockSpec((SUBLANES, cb), lambda i, j, hp, hn: (jnp.maximum(i * r8 - 1, 0), j)),
                  pl.BlockSpec((tb, cb), lambda i, j, hp, hn: (i, j)),
                  pl.BlockSpec((SUBLANES, cb), lambda i, j, hp, hn: (jnp.minimum((i + 1) * r8, nb8 - 1), j)),
                  pl.BlockSpec((DN_CONV, cb), lambda i, j, hp, hn: (0, j))],
        out_specs=pl.BlockSpec((tb, cb), lambda i, j, hp, hn: (i, j)))
    return pl.pallas_call(
        functools.partial(_conv_kernel, kind_blocks=qk_width // cb, q_scale=dk ** -0.5),
        grid_spec=grid_spec,
        out_shape=jax.ShapeDtypeStruct((t, c), F32),
        compiler_params=_cparams("parallel", "parallel"), name="dn_conv",
    )(seq_lens["has_prev"], seq_lens["has_next"], qkvz, qkvz, qkvz, conv_w)


def _bdot(a, b):
    return _dot(a.astype(BF16), b.astype(BF16))


def _bdot_nt(a, b):
    return _dot_nt(a.astype(BF16), b.astype(BF16))


def _split(a):
    hi = a.astype(BF16)
    return hi, (a - hi.astype(F32)).astype(BF16)


def _dot3(a, b):
    return _dot(a[0], b[0]) + (_dot(a[0], b[1]) + _dot(a[1], b[0]))


def _unit_tri_inverses(mats):
    n = mats[0].shape[0]
    eye = (lax.broadcasted_iota(jnp.int32, (n, n), 0) == lax.broadcasted_iota(jnp.int32, (n, n), 1)).astype(F32)
    ts = [eye - a for a in mats]
    xs = [_split(-a) for a in mats]
    p = 2
    while p < n:
        xs = [_split(_dot3(x, x)) for x in xs]
        ts = [t + _dot3(_split(t), x) for t, x in zip(ts, xs)]
        p *= 2
    return ts


def _delta_kernel(blk_ref, seq_ref, first_ref, last_ref, q_ref, k_ref, v_ref, pc_ref, pr_ref, s0_ref, o_ref, sf_ref,
                  s_scr, *, hb, cg, reverse):
    n = pl.program_id(1)

    @pl.when(first_ref[n] == 1)
    def _():
        s_scr[...] = s0_ref[...]

    c = DN_CHUNK
    row = lax.broadcasted_iota(jnp.int32, (c, c), 0)
    col = lax.broadcasted_iota(jnp.int32, (c, c), 1)
    incl = (row <= col) if reverse else (row >= col)
    strict = (row < col) if reverse else (row > col)
    last = 0 if reverse else c - 1
    order = range(cg - 1, -1, -1) if reverse else range(cg)

    shared = {}
    for ci in order:
        rows = slice(ci * c, (ci + 1) * c)
        for kh in range(hb // 2):
            q = q_ref[rows, kh * LANES:(kh + 1) * LANES]
            k = k_ref[rows, kh * LANES:(kh + 1) * LANES]
            shared[ci, kh] = dict(q=q, k=k, kk=_bdot_nt(k, k), qk=_bdot_nt(q, k), kt=k.T)

    probs = [(ci, hh) for ci in order for hh in range(hb)]
    pre = {}
    for ci, hh in probs:
        pc = pc_ref[ci * c:(ci + 1) * c, :]
        pr = pr_ref[ci]
        beta = pc[:, hh:hh + 1]
        g_col = pc[:, hb + hh:hb + hh + 1]
        g_row = pr[hb + hh:hb + hh + 1, :]
        g_last = g_row[:, last:last + 1]
        decay = jnp.where(incl, jnp.exp(jnp.where(incl, g_col - g_row, 0.0)), 0.0)
        pre[ci, hh] = dict(beta=beta, e_g=jnp.exp(g_col), decay=decay, gl=jnp.exp(g_last),
                           kscale=jnp.exp(g_last - g_row),
                           a=jnp.where(strict, beta * decay * shared[ci, hh // 2]["kk"], 0.0))
    tinvs = _unit_tri_inverses([pre[p]["a"] for p in probs])

    prep = {}
    for (ci, hh), tinv in zip(probs, tinvs):
        sh, pp = shared[ci, hh // 2], pre[ci, hh]
        v = v_ref[ci * c:(ci + 1) * c, hh * LANES:(hh + 1) * LANES]
        sol = _bdot(tinv, jnp.concatenate([pp["beta"] * v, (pp["beta"] * pp["e_g"]) * sh["k"]], axis=1))
        prep[ci, hh] = dict(
            u=sol[:, :LANES],
            wk_qg=jnp.concatenate([sol[:, LANES:], sh["q"] * pp["e_g"]], axis=0).astype(BF16),
            aqk=(pp["decay"] * sh["qk"]).astype(BF16),
            kdt=(sh["kt"] * pp["kscale"]).astype(BF16),
            gl=pp["gl"])

    states = [s_scr[hh] for hh in range(hb)]
    for ci in order:
        boths = [_dot(prep[ci, hh]["wk_qg"], states[hh].astype(BF16)) for hh in range(hb)]
        ws = [(prep[ci, hh]["u"] - boths[hh][:c]).astype(BF16) for hh in range(hb)]
        for hh in range(hb):
            o_ref[ci * c:(ci + 1) * c, hh * LANES:(hh + 1) * LANES] = (
                boths[hh][c:] + _dot(prep[ci, hh]["aqk"], ws[hh]))
        states = [prep[ci, hh]["gl"] * states[hh] + _dot(prep[ci, hh]["kdt"], ws[hh]) for hh in range(hb)]
    for hh in range(hb):
        s_scr[hh] = states[hh]

    @pl.when(last_ref[n] == 1)
    def _():
        sf_ref[...] = s_scr[...]


def _delta_scan(qkvc, p_col, p_row, s0, seq_lens, *, hk, hv, dk, dv, direction, hb, cg):
    assert dk == LANES and dv == LANES and hv == 2 * hk
    c = DN_CHUNK * cg
    hg = hv // hb
    reverse = direction == 1
    blk, seq, first, last = [], [], [], []
    start = 0
    for s, length in enumerate(seq_lens):
        nb = length // c
        ids = list(range(start, start + nb))
        blk += ids[::-1] if reverse else ids
        seq += [s] * nb
        first += [1] + [0] * (nb - 1)
        last += [0] * (nb - 1) + [1]
        start += nb
    tables = [jnp.asarray(np.asarray(a, np.int32)) for a in (blk, seq, first, last)]
    t = start * c

    qw = (hb // 2) * dk
    vw = hb * dv
    state_spec = pl.BlockSpec((None, hb, dk, dv), lambda g, n, bt, st, ft, lt: (st[n], g, 0, 0))
    grid_spec = pltpu.PrefetchScalarGridSpec(
        num_scalar_prefetch=4, grid=(hg, len(blk)),
        in_specs=[pl.BlockSpec((c, qw), lambda g, n, bt, st, ft, lt: (bt[n], g)),
                  pl.BlockSpec((c, qw), lambda g, n, bt, st, ft, lt: (bt[n], hk * dk // qw + g)),
                  pl.BlockSpec((c, vw), lambda g, n, bt, st, ft, lt: (bt[n], 2 * hk * dk // vw + g)),
                  pl.BlockSpec((None, None, c, 2 * hb), lambda g, n, bt, st, ft, lt: (g, direction, bt[n], 0)),
                  pl.BlockSpec((None, None, cg, 2 * hb, DN_CHUNK),
                               lambda g, n, bt, st, ft, lt: (g, direction, bt[n], 0, 0)),
                  state_spec],
        out_specs=[pl.BlockSpec((c, vw), lambda g, n, bt, st, ft, lt: (bt[n], g)), state_spec],
        scratch_shapes=[pltpu.VMEM((hb, dk, dv), F32)])
    return pl.pallas_call(
        functools.partial(_delta_kernel, hb=hb, cg=cg, reverse=reverse),
        grid_spec=grid_spec,
        out_shape=[jax.ShapeDtypeStruct((t, hv * dv), F32), jax.ShapeDtypeStruct(s0.shape, F32)],
        compiler_params=_cparams("parallel", "arbitrary"), name="dn_scan",
    )(*tables, qkvc, qkvc, qkvc, p_col, p_row, s0)


def _gated_norm_kernel(of_ref, ob_ref, z_ref, g_ref, o_ref):
    for c in range(o_ref.shape[1] // LANES):
        sl = slice(c * LANES, (c + 1) * LANES)
        o = of_ref[:, sl] + ob_ref[:, sl]
        o = o * lax.rsqrt(jnp.mean(o * o, axis=-1, keepdims=True) + EPS) * g_ref[...]
        o_ref[:, sl] = (o * _silu(z_ref[:, sl])).astype(o_ref.dtype)


def _gated_norm(o_f, o_b, qkvz, z_col0, norm_g):
    t, w = o_f.shape
    tb = _pick(t, 256)
    cb = _pick(math.gcd(w, z_col0), 1024)
    tok = pl.BlockSpec((tb, cb), lambda i, j: (i, j))
    return pl.pallas_call(
        _gated_norm_kernel, grid=(t // tb, w // cb),
        in_specs=[tok, tok, pl.BlockSpec((tb, cb), lambda i, j: (i, z_col0 // cb + j)),
                  pl.BlockSpec((1, LANES), lambda i, j: (0, 0))],
        out_specs=tok, out_shape=jax.ShapeDtypeStruct((t, w), BF16),
        compiler_params=_cparams("parallel", "parallel"), name="dn_gated_norm",
    )(o_f, o_b, qkvz, norm_g.reshape(1, LANES).astype(F32))


def _na_ctx_kernel(q_ref, k_ref, v_ref, o_ref, *, scale):
    q = (q_ref[...] * scale).astype(BF16)
    s = _dot_nt(q, k_ref[...].astype(BF16))
    p = jnp.exp(s - jnp.max(s, axis=-1, keepdims=True))
    o = _dot(p.astype(BF16), v_ref[...].astype(BF16)) / jnp.sum(p, axis=-1, keepdims=True)
    o_ref[...] = o.astype(o_ref.dtype)


def _na_context(qkv, n_seq, seq_len, heads):
    blk = lambda off: pl.BlockSpec((seq_len, LANES), lambda b, h: (b, off * heads + h))
    return pl.pallas_call(
        functools.partial(_na_ctx_kernel, scale=LANES ** -0.5),
        grid=(n_seq, heads), in_specs=[blk(0), blk(1), blk(2)],
        out_specs=pl.BlockSpec((seq_len, LANES), lambda b, h: (b, h)),
        out_shape=jax.ShapeDtypeStruct((n_seq * seq_len, heads * LANES), BF16),
        compiler_params=_cparams("parallel", "parallel"), name="na_ctx",
    )(qkv, qkv, qkv)


def _na_lat_kernel(q_ref, k_ref, v_ref, kc_ref, vc_ref, bias_ref, o_ref, kb_scr, vb_scr, *, rows, wr, scale):
    kb_scr[...] = k_ref[...].astype(BF16)
    vb_scr[...] = v_ref[...].astype(BF16)
    kc = kc_ref[...].astype(BF16)
    vc = vc_ref[...].astype(BF16)
    w = GRID_W

    def body(r, carry):
        r0 = jnp.clip(r - wr // 2, 0, rows - wr)
        q = (q_ref[pl.ds(pl.multiple_of(r * w, w), w), :] * scale).astype(BF16)
        win = pl.ds(pl.multiple_of(r0 * w, w), wr * w)
        s_lat = _dot_nt(q, kb_scr[win, :])
        d0 = r0 - r + WIN_R - 1
        s_lat = s_lat + jnp.concatenate([bias_ref[d0 + 2 * p] for p in range(wr // 2)], axis=1)
        s_ctx = _dot_nt(q, kc)
        m = jnp.maximum(jnp.max(s_lat, axis=-1, keepdims=True), jnp.max(s_ctx, axis=-1, keepdims=True))
        p_lat = jnp.exp(s_lat - m)
        p_ctx = jnp.exp(s_ctx - m)
        den = jnp.sum(p_lat, axis=-1, keepdims=True) + jnp.sum(p_ctx, axis=-1, keepdims=True)
        o = _dot(p_lat.astype(BF16), vb_scr[win, :]) + _dot(p_ctx.astype(BF16), vc)
        o_ref[pl.ds(pl.multiple_of(r * w, w), w), :] = (o / den).astype(o_ref.dtype)
        return carry

    lax.fori_loop(0, rows, body, 0)


def _na_bias_pairs(rpb):
    col = np.arange(GRID_W)
    c0 = np.clip(col - WIN_C // 2, 0, GRID_W - WIN_C)
    col_ok = (col[None, :] >= c0[:, None]) & (col[None, :] < c0[:, None] + WIN_C)
    dc_idx = np.clip(col[None, :] - col[:, None], 1 - WIN_C, WIN_C - 1) + WIN_C - 1
    bias = jnp.where(col_ok[None, None], rpb.astype(F32)[:, :, dc_idx], NEG)
    return jnp.concatenate([bias[:, :-1], bias[:, 1:]], axis=-1)


def _na_latent(qkv, cache_k, cache_v, rpb, row0, n_seq, seq_len, heads):
    rows = seq_len // GRID_W
    wr = min(WIN_R, rows)
    assert wr % 2 == 0 and row0 % seq_len == 0
    b0 = row0 // seq_len
    past = cache_k.shape[1]
    bias = _na_bias_pairs(rpb)
    blk = lambda off: pl.BlockSpec((seq_len, LANES), lambda b, h: (b0 + b, off * heads + h))
    cache = pl.BlockSpec((None, past, LANES), lambda b, h: (b, 0, h))
    return pl.pallas_call(
        functools.partial(_na_lat_kernel, rows=rows, wr=wr, scale=LANES ** -0.5),
        grid=(n_seq, heads),
        in_specs=[blk(0), blk(1), blk(2), cache, cache,
                  pl.BlockSpec((None,) + bias.shape[1:], lambda b, h: (h, 0, 0, 0))],
        out_specs=pl.BlockSpec((seq_len, LANES), lambda b, h: (b, h)),
        out_shape=jax.ShapeDtypeStruct((n_seq * seq_len, heads * LANES), BF16),
        scratch_shapes=[pltpu.VMEM((seq_len, LANES), BF16), pltpu.VMEM((seq_len, LANES), BF16)],
        compiler_params=_cparams("parallel", "parallel"), name="na_lat",
    )(qkv, qkv, qkv, cache_k, cache_v, bias)


def _top16(s, row):
    n = s.shape[0]
    vals = []
    rank = jnp.full(s.shape, PEER_TOPK, jnp.int32)
    for r in range(PEER_TOPK):
        m = jnp.max(s, axis=0, keepdims=True)
        idx = jnp.min(jnp.where(s == m, row, n), axis=0, keepdims=True)
        hit = row == idx
        vals.append(m)
        rank = jnp.where(hit, r, rank)
        s = jnp.where(hit, -jnp.inf, s)
    return jnp.concatenate(vals, axis=0), rank


def _peer_topk_kernel(q_ref, keys_ref, n_ref, e1_ref, b_ref, e2_ref, *, heads, half):
    tb = q_ref.shape[0]
    nk = keys_ref.shape[2]
    kk = PEER_TOPK
    assert kk == 2 * SUBLANES
    row = lax.broadcasted_iota(jnp.int32, (nk, tb), 0)
    row_k = lax.broadcasted_iota(jnp.int32, (kk, tb), 0)
    sub = lax.broadcasted_iota(jnp.int32, (SUBLANES, tb), 0)
    flat = jnp.concatenate([row_k] + [a * kk + sub for a in range(1, SUBLANES)]
                           + [(SUBLANES + g * SUBLANES + sub) * kk for g in range(kk // SUBLANES - 1)], axis=0)
    for h in range(heads):
        q1 = q_ref[:, (2 * h) * half:(2 * h + 1) * half]
        q2 = q_ref[:, (2 * h + 1) * half:(2 * h + 2) * half]
        s1 = _dot_nt(keys_ref[h, 0], q1, HI)
        s2 = _dot_nt(keys_ref[h, 1], q2, HI)
        v1, rank1 = _top16(s1, row)
        v2, rank2 = _top16(s2, row)
        cand = jnp.concatenate([v1[0:1, :] + v2] + [v1[a:a + 1, :] + v2[:SUBLANES] for a in range(1, SUBLANES)]
                               + [v1[SUBLANES:, :] + v2[0:1, :]], axis=0)
        cnt = jnp.zeros((kk, tb), jnp.int32)
        zsum = jnp.zeros((1, tb), F32)
        best0 = None
        for r in range(kk):
            m = jnp.max(cand, axis=0, keepdims=True)
            idx = jnp.min(jnp.where(cand == m, flat, kk * kk), axis=0, keepdims=True)
            best0 = m if r == 0 else best0
            zsum = zsum + jnp.exp(m - best0)
            cnt = cnt + (row_k == idx // kk).astype(jnp.int32)
            cand = jnp.where(flat == idx, -jnp.inf, cand)
        n_i = jnp.zeros((nk, tb), jnp.int32)
        for a in range(kk):
            n_i = jnp.where(rank1 == a, cnt[a:a + 1, :], n_i)
        n_ref[h] = n_i.astype(F32)
        e1_ref[h] = jnp.exp(s1 - v1[0:1, :]) / zsum
        b_ref[h] = pltpu.bitcast(rank2.astype(F32).astype(BF16), jnp.uint32)
        e2_ref[h] = pltpu.bitcast(jnp.exp(s2 - v2[0:1, :]).astype(BF16), jnp.uint32)


def _peer_topk(q, keys):
    t = q.shape[0]
    heads, _, nk, half = keys.shape
    tb = _pick(t, 256)
    out = [jax.ShapeDtypeStruct((heads, nk, t), F32)] * 2 + [jax.ShapeDtypeStruct((heads, nk // 2, t), jnp.uint32)] * 2
    spec = pl.BlockSpec((heads, nk, tb), lambda i: (0, 0, i))
    pspec = pl.BlockSpec((heads, nk // 2, tb), lambda i: (0, 0, i))
    return pl.pallas_call(
        functools.partial(_peer_topk_kernel, heads=heads, half=half),
        grid=(t // tb,),
        in_specs=[pl.BlockSpec((tb, q.shape[1]), lambda i: (i, 0)),
                  pl.BlockSpec(keys.shape, lambda i: (0, 0, 0, 0))],
        out_specs=[spec, spec, pspec, pspec], out_shape=out,
        compiler_params=_cparams("parallel"), name="peer_topk",
    )(q, keys)


def _gelu(x):
    return 0.5 * x * (1.0 + lax.erf(x * (2.0 ** -0.5)))


def _peer_dense_kernel(xt_ref, u_ref, vt_ref, n_ref, e1_ref, b_ref, e2_ref, o_ref, acc_scr, a_scr, w_scr,
                       cnt_scr, e1_scr, *, heads, ib, tw):
    j = pl.program_id(1)

    @pl.when(j == 0)
    def _():
        acc_scr[...] = jnp.zeros(acc_scr.shape, F32)

    nk = 2 * b_ref.shape[1]
    tb = xt_ref.shape[1]
    n_slab = tb // tw

    def scores(s):
        a_scr[s] = _dot(u_ref[...], xt_ref[:, s * tw:(s + 1) * tw])

    for h in range(heads):
        for ii in range(ib):
            cnt_scr[h, ii] = jnp.broadcast_to(n_ref[h, ii:ii + 1, :], (BF16_ROWS, tb)).astype(BF16)
            e1_scr[h, ii] = jnp.broadcast_to(e1_ref[h, ii:ii + 1, :], (BF16_ROWS, tb)).astype(BF16)

    def gate_tiles(s):
        rep = nk // BF16_ROWS
        for ii in range(ib):
            rows = slice(ii * nk, (ii + 1) * nk)
            for tc in range(tw // LANES):
                lanes = slice(s * tw + tc * LANES, s * tw + (tc + 1) * LANES)
                g = jnp.zeros((rep, BF16_ROWS, LANES), BF16)
                for h in range(heads):
                    cnt = cnt_scr[h, ii, :, lanes][None]
                    e1 = e1_scr[h, ii, :, lanes][None]
                    b = pltpu.bitcast(b_ref[h, :, lanes], BF16).reshape(g.shape)
                    e2 = pltpu.bitcast(e2_ref[h, :, lanes], BF16).reshape(g.shape)
                    g = g + jnp.where(b < cnt, e1 * e2, jnp.zeros((), BF16))
                a = a_scr[s, rows, tc * LANES:(tc + 1) * LANES]
                w_scr[s, rows, tc * LANES:(tc + 1) * LANES] = g.reshape(nk, LANES) * _gelu(a).astype(BF16)

    scores(0)
    for s in range(n_slab):
        if s + 1 < n_slab:
            scores(s + 1)
        gate_tiles(s)
        acc_scr[:, s * tw:(s + 1) * tw] += _dot(vt_ref[...], w_scr[s])

    @pl.when(j == pl.num_programs(1) - 1)
    def _():
        o_ref[...] = acc_scr[...].T


def _peer_dense(ht, u_tab, vt_tab, sel):
    d, t = ht.shape
    ne = u_tab.shape[0]
    n_i, e1, b_j, e2 = sel
    heads, nk, _ = n_i.shape
    tb = _pick(t, 512)
    tw = _pick(tb, MXU_WIDTH)
    ib = SUBLANES
    eb = ib * nk
    n_i = n_i.reshape(heads, nk // ib, ib, t)
    e1 = e1.reshape(heads, nk // ib, ib, t)
    row_spec = pl.BlockSpec((heads, None, ib, tb), lambda i, j: (0, j, 0, i))
    col_spec = pl.BlockSpec((heads, nk // 2, tb), lambda i, j: (0, 0, i))
    return pl.pallas_call(
        functools.partial(_peer_dense_kernel, heads=heads, ib=ib, tw=tw),
        grid=(t // tb, ne // eb),
        in_specs=[pl.BlockSpec((d, tb), lambda i, j: (0, i)),
                  pl.BlockSpec((eb, d), lambda i, j: (j, 0)),
                  pl.BlockSpec((d, eb), lambda i, j: (0, j)),
                  row_spec, row_spec, col_spec, col_spec],
        out_specs=pl.BlockSpec((tb, d), lambda i, j: (i, 0)),
        out_shape=jax.ShapeDtypeStruct((t, d), F32),
        scratch_shapes=[pltpu.VMEM((d, tb), F32), pltpu.VMEM((tb // tw, eb, tw), F32),
                        pltpu.VMEM((tb // tw, eb, tw), BF16),
                        pltpu.VMEM((heads, ib, BF16_ROWS, tb), BF16), pltpu.VMEM((heads, ib, BF16_ROWS, tb), BF16)],
        compiler_params=_cparams("parallel", "arbitrary"), name="peer_dense",
    )(ht, u_tab, vt_tab, n_i, e1, b_j, e2)


def _peer(h, ht, w_q, keys, u_tab, v_tab):
    q = _matmul(h, w_q.astype(BF16))
    sel = _peer_topk(q, keys.astype(F32))
    return _peer_dense(ht, u_tab.astype(BF16), v_tab.T.astype(BF16), sel)


def _seq_flags(groups, n_ctx_seq, tb):
    ctx_len = groups.ctx_tokens // n_ctx_seq
    starts = np.concatenate([np.arange(n_ctx_seq) * ctx_len,
                             groups.ctx_tokens + np.arange(groups.n_lat) * groups.lat_len])
    ends = np.concatenate([starts[1:], [groups.total]])
    blk = np.arange(groups.total // tb) * tb
    has_prev = ~np.isin(blk, starts)
    has_next = ~np.isin(blk + tb, ends)
    return {"tb": tb, "has_prev": jnp.asarray(has_prev, jnp.int32), "has_next": jnp.asarray(has_next, jnp.int32)}


def _deltanet(h, x, gate, groups, n_ctx_seq, w_in, conv_w, a_log, dt_bias, norm_g, w_o, state_delta):
    hv = a_log.shape[-1]
    dk, dv = state_delta.shape[-2:]
    qkv_w = conv_w.shape[1]
    v_w = hv * dv
    hk = (qkv_w - v_w) // (2 * dk)
    ctx_len = groups.ctx_tokens // n_ctx_seq
    t = groups.total
    c = DN_CHUNK

    qkvz = _matmul(h, w_in.astype(BF16), n_cols=qkv_w + v_w)
    p = _gates(h, w_in[:, qkv_w + v_w:].astype(F32), a_log, dt_bias)
    tb = _pick(math.gcd(ctx_len, groups.lat_len), 256)
    qkvc = _short_conv(qkvz, conv_w.astype(F32), _seq_flags(groups, n_ctx_seq, tb), hk * dk, dk)

    hb = min(DN_HEADS_PER_STEP, hv)
    cg = DN_CHUNKS_PER_STEP
    hg = hv // hb
    p5 = p.reshape(t, 2, 2, hg, hb).transpose(3, 2, 0, 1, 4).reshape(hg, 2, t, 2 * hb)
    p_row = p5.reshape(hg, 2, t // c, c, 2 * hb).transpose(0, 1, 2, 4, 3)
    seq_lens = [ctx_len] * n_ctx_seq + [groups.lat_len] * groups.n_lat
    zero_state = jnp.zeros((n_ctx_seq, hv, dk, dv), F32)
    outs, states = [], []
    for direction in range(2):
        s0 = jnp.concatenate([zero_state, state_delta[:, direction].astype(F32)], axis=0)
        o, s_fin = _delta_scan(qkvc, p5, p_row, s0, seq_lens, hk=hk, hv=hv, dk=dk, dv=dv, direction=direction,
                               hb=hb, cg=cg)
        outs.append(o)
        states.append(s_fin[:n_ctx_seq])
    og = _gated_norm(outs[0], outs[1], qkvz, qkv_w, norm_g)
    x = _matmul(og, w_o.astype(BF16), groups=groups, res=x, gate=gate)
    return x, jnp.stack(states, axis=1)


def _natten(h, x, gate, groups, n_ctx_seq, w_qkv, rpb, w_o, cache_k, cache_v):
    d = h.shape[1]
    heads = d // LANES
    ctx_len = groups.ctx_tokens // n_ctx_seq
    qkv = _matmul(h, w_qkv.astype(BF16))
    o_c = _na_context(qkv, n_ctx_seq, ctx_len, heads)
    n_lat, past = cache_k.shape[:2]
    o_l = _na_latent(qkv, cache_k.reshape(n_lat, past, d), cache_v.reshape(n_lat, past, d), rpb,
                     groups.ctx_tokens, groups.n_lat, groups.lat_len, heads)
    x = _matmul(jnp.concatenate([o_c, o_l], axis=0), w_o.astype(BF16), groups=groups, res=x, gate=gate)
    k_c = qkv[:groups.ctx_tokens, d:2 * d].reshape(n_ctx_seq, ctx_len, heads, LANES)
    v_c = qkv[:groups.ctx_tokens, 2 * d:].reshape(n_ctx_seq, ctx_len, heads, LANES)
    return x, k_c, v_c


def kernel(x_prompt, x_sample, c, state_delta, cache_k, cache_v, c_ctx, ada_w, ada_b, norm1_g, norm2_g, final_g,
           dn_w_in, dn_conv_w, dn_a_log, dn_dt_bias, dn_norm_g, dn_w_o, na_w_qkv, na_rpb, na_w_o,
           peer_w_q, peer_keys, peer_u, peer_v):
    n_ctx_seq, ctx_len, d = x_prompt.shape
    n_lat, lat_len, _ = x_sample.shape
    depth = ada_w.shape[0]
    groups = _Groups(n_ctx_seq * ctx_len, n_lat, lat_len)
    n_grp = 1 + n_lat

    x = jnp.concatenate([x_prompt.reshape(-1, d), x_sample.reshape(-1, d)], axis=0).astype(F32)
    pad = (-n_grp) % SUBLANES
    cvecs = jnp.concatenate([c_ctx[None], c, jnp.zeros((pad, d), c.dtype)], axis=0).astype(F32)
    mods = _adaln(cvecs, ada_w.astype(F32), ada_b.astype(F32))[:, :n_grp]
    mods = mods.reshape(depth, n_grp, 6, 1, d).transpose(0, 2, 1, 3, 4)

    states, ctx_k, ctx_v = [], [], []
    delta = gate = None
    for i in range(depth):
        sh1, sc1, g1, sh2, sc2, g2 = (mods[i, k] for k in range(6))
        j = i // 2
        x, h = _normmod(x, norm1_g[i].astype(F32), groups, delta, gate, sh1, sc1)
        if i % 2 == 0:
            x, s_fin = _deltanet(h, x, g1, groups, n_ctx_seq, dn_w_in[j], dn_conv_w[j], dn_a_log[j], dn_dt_bias[j],
                                 dn_norm_g[j], dn_w_o[j], state_delta[:, j])
            states.append(s_fin)
        else:
            x, k_c, v_c = _natten(h, x, g1, groups, n_ctx_seq, na_w_qkv[j], na_rpb[j], na_w_o[j],
                                  cache_k[:, j], cache_v[:, j])
            ctx_k.append(k_c)
            ctx_v.append(v_c)
        x, h, ht = _normmod(x, norm2_g[i].astype(F32), groups, None, None, sh2, sc2, transposed=True)
        delta, gate = _peer(h, ht, peer_w_q[i], peer_keys[i], peer_u[i], peer_v[i]), g2
    final = functools.partial(_normmod, x, final_g.astype(F32), groups, delta, gate, out_dtype=F32)
    y_prompt = final(rows=(0, groups.ctx_tokens)).reshape(x_prompt.shape)
    y_sample = final(rows=(groups.ctx_tokens, n_lat * lat_len)).reshape(x_sample.shape)
    return (y_prompt, y_sample, jnp.stack(states, axis=1), jnp.stack(ctx_k, axis=1), jnp.stack(ctx_v, axis=1))
```

```python
import functools
import math

import jax
import jax.numpy as jnp
import numpy as np
from jax import lax
from jax.experimental import pallas as pl
from jax.experimental.pallas import tpu as pltpu

GRID_W = 64
WIN_R = 8
WIN_C = 16
DN_CHUNK = 64
DN_CONV = 5
PEER_TOPK = 16
EPS = 1e-6
DN_HEADS_PER_STEP = 8
DN_CHUNKS_PER_STEP = 2
NA_ROWS_PER_STEP = 4

LANES = 128
SUBLANES = 8
BF16_ROWS = 16
MXU_WIDTH = 256
VMEM_LIMIT = 56 * 1024 * 1024

F32 = jnp.float32
BF16 = jnp.bfloat16
HI = lax.Precision.HIGHEST
NEG = -1e30


def _cparams(*sem):
    return pltpu.CompilerParams(dimension_semantics=sem, vmem_limit_bytes=VMEM_LIMIT)


def _dot(a, b, precision=None):
    return jnp.dot(a, b, preferred_element_type=F32, precision=precision)


def _dot_nt(a, b, precision=None):
    return lax.dot_general(a, b, (((1,), (1,)), ((), ())), preferred_element_type=F32, precision=precision)


def _silu(x):
    return x * (1.0 / (1.0 + jnp.exp(-x)))


def _pick(n, pref):
    b = min(n, pref)
    while n % b:
        b -= 1
    return b


def _adaln_kernel(c_ref, w_ref, b_ref, o_ref):
    o_ref[...] = _dot(_silu(c_ref[...]), w_ref[...], HI) + b_ref[...]


def _adaln(cvecs, ada_w, ada_b):
    depth, d, n6 = ada_w.shape
    r = cvecs.shape[0]
    tn = _pick(n6, 1024)
    return pl.pallas_call(
        _adaln_kernel,
        grid=(depth, n6 // tn),
        in_specs=[pl.BlockSpec((r, d), lambda i, j: (0, 0)),
                  pl.BlockSpec((None, d, tn), lambda i, j: (i, 0, j)),
                  pl.BlockSpec((None, 1, tn), lambda i, j: (i, 0, j))],
        out_specs=pl.BlockSpec((None, r, tn), lambda i, j: (i, 0, j)),
        out_shape=jax.ShapeDtypeStruct((depth, r, n6), F32),
        compiler_params=_cparams("parallel", "parallel"),
        name="adaln",
    )(cvecs, ada_w, ada_b.reshape(depth, 1, n6))


def _normmod_kernel(*refs, has_delta, has_mod, emit_x, emit_t):
    it = iter(refs)
    x_ref = next(it)
    if has_delta:
        d_ref, gate_ref = next(it), next(it)
    g_ref = next(it)
    if has_mod:
        sh_ref, sc_ref = next(it), next(it)
    if emit_x:
        xo_ref = next(it)
    h_ref = next(it)
    x = x_ref[...]
    if has_delta:
        x = x + gate_ref[...] * d_ref[...]
    if emit_x:
        xo_ref[...] = x
    y = x * lax.rsqrt(jnp.mean(x * x, axis=-1, keepdims=True) + EPS) * g_ref[...]
    if has_mod:
        y = y * (1.0 + sc_ref[...]) + sh_ref[...]
    h_ref[...] = y.astype(h_ref.dtype)
    if emit_t:
        next(it)[...] = y.T.astype(h_ref.dtype)


def _normmod(x, gain, groups, delta=None, gate=None, shift=None, scale=None, out_dtype=BF16, transposed=False,
             rows=None):
    t, d = x.shape
    tb = groups.block(256)
    row0, t_out = (0, t) if rows is None else rows
    b0 = row0 // tb
    has_delta, has_mod = delta is not None, shift is not None
    emit_x = has_delta and rows is None
    tok = pl.BlockSpec((tb, d), lambda i: (b0 + i, 0))
    out_tok = pl.BlockSpec((tb, d), lambda i: (i, 0))
    grp = pl.BlockSpec((None, 1, d), lambda i: (groups.of_block(b0 + i, tb), 0, 0))
    args, specs = [x], [tok]
    if has_delta:
        args += [delta, gate]
        specs += [tok, grp]
    args.append(gain.reshape(1, d))
    specs.append(pl.BlockSpec((1, d), lambda i: (0, 0)))
    if has_mod:
        args += [shift, scale]
        specs += [grp, grp]
    out_shape, out_specs = [], []
    if emit_x:
        out_shape.append(jax.ShapeDtypeStruct((t, d), F32))
        out_specs.append(out_tok)
    out_shape.append(jax.ShapeDtypeStruct((t_out, d), out_dtype))
    out_specs.append(out_tok)
    if transposed:
        out_shape.append(jax.ShapeDtypeStruct((d, t_out), out_dtype))
        out_specs.append(pl.BlockSpec((d, tb), lambda i: (0, i)))
    res = pl.pallas_call(
        functools.partial(_normmod_kernel, has_delta=has_delta, has_mod=has_mod, emit_x=emit_x,
                          emit_t=transposed),
        grid=(t_out // tb,), in_specs=specs, out_specs=out_specs, out_shape=out_shape,
        compiler_params=_cparams("parallel"), name="normmod",
    )(*args)
    if rows is not None:
        return res[0]
    res = list(res) if has_delta else [x] + list(res)
    return tuple(res)


class _Groups:
    def __init__(self, ctx_tokens, n_lat, lat_len):
        self.ctx_tokens, self.n_lat, self.lat_len = ctx_tokens, n_lat, lat_len
        self.total = ctx_tokens + n_lat * lat_len

    def block(self, pref):
        return _pick(math.gcd(self.ctx_tokens, self.lat_len), pref)

    def of_block(self, i, tb):
        nctx = self.ctx_tokens // tb
        return jnp.where(i < nctx, 0, 1 + (i - nctx) // (self.lat_len // tb))


def _matmul_kernel(*refs, has_res):
    if has_res:
        a_ref, w_ref, r_ref, g_ref, o_ref = refs
        o_ref[...] = r_ref[...] + g_ref[...] * _dot(a_ref[...], w_ref[...])
    else:
        a_ref, w_ref, o_ref = refs
        o_ref[...] = _dot(a_ref[...], w_ref[...]).astype(o_ref.dtype)


def _matmul(a, w, groups=None, res=None, gate=None, out_dtype=F32, n_cols=None, col0=0):
    m, k = a.shape
    n = w.shape[1] if n_cols is None else n_cols
    tm = _pick(m, 1024) if groups is None else groups.block(1024)
    tn = _pick(math.gcd(n, col0) if col0 else n, 512)
    cb0 = col0 // tn
    has_res = res is not None
    args = [a, w]
    specs = [pl.BlockSpec((tm, k), lambda i, j: (i, 0)), pl.BlockSpec((k, tn), lambda i, j: (0, cb0 + j))]
    if has_res:
        args += [res, gate]
        specs += [pl.BlockSpec((tm, tn), lambda i, j: (i, j)),
                  pl.BlockSpec((None, 1, tn), lambda i, j: (groups.of_block(i, tm), 0, j))]
    return pl.pallas_call(
        functools.partial(_matmul_kernel, has_res=has_res),
        grid=(m // tm, n // tn), in_specs=specs,
        out_specs=pl.BlockSpec((tm, tn), lambda i, j: (i, j)),
        out_shape=jax.ShapeDtypeStruct((m, n), out_dtype),
        compiler_params=_cparams("parallel", "parallel"), name="matmul",
    )(*args)


def _gates_kernel(h_ref, w_ref, alog_ref, dtb_ref, p_ref, *, hv, n_chunks):
    lane = lax.broadcasted_iota(jnp.int32, (DN_CHUNK, 4 * hv), 1)
    row = lax.broadcasted_iota(jnp.int32, (DN_CHUNK, DN_CHUNK), 0)
    col = lax.broadcasted_iota(jnp.int32, (DN_CHUNK, DN_CHUNK), 1)
    tri_lo = (row >= col).astype(F32)
    tri_up = (row <= col).astype(F32)
    bwd = ((lane // hv) % 2) == 1
    for c in range(n_chunks):
        sl = slice(c * DN_CHUNK, (c + 1) * DN_CHUNK)
        ba = _dot(h_ref[sl, :].astype(F32), w_ref[...], HI)
        beta = 1.0 / (1.0 + jnp.exp(-ba))
        z = ba + dtb_ref[...]
        softplus = jnp.maximum(z, 0.0) + jnp.log(1.0 + jnp.exp(-jnp.abs(z)))
        g = -jnp.exp(alog_ref[...]) * softplus
        cum = jnp.where(bwd, _dot(tri_up, g, HI), _dot(tri_lo, g, HI))
        p_ref[sl, :] = jnp.where(lane < 2 * hv, beta, cum)


def _gates(h, w_ba, a_log, dt_bias):
    t, d = h.shape
    hv = a_log.shape[-1]
    tb = _pick(t, 256)
    zeros = jnp.zeros((1, 2 * hv), F32)
    alog_row = jnp.concatenate([zeros, a_log.reshape(1, 2 * hv).astype(F32)], axis=1)
    dtb_row = jnp.concatenate([zeros, dt_bias.reshape(1, 2 * hv).astype(F32)], axis=1)
    return pl.pallas_call(
        functools.partial(_gates_kernel, hv=hv, n_chunks=tb // DN_CHUNK),
        grid=(t // tb,),
        in_specs=[pl.BlockSpec((tb, d), lambda i: (i, 0)),
                  pl.BlockSpec((d, 4 * hv), lambda i: (0, 0)),
                  pl.BlockSpec((1, 4 * hv), lambda i: (0, 0)),
                  pl.BlockSpec((1, 4 * hv), lambda i: (0, 0))],
        out_specs=pl.BlockSpec((tb, 4 * hv), lambda i: (i, 0)),
        out_shape=jax.ShapeDtypeStruct((t, 4 * hv), F32),
        compiler_params=_cparams("parallel"), name="dn_gates",
    )(h, w_ba, alog_row, dtb_row)


def _conv_kernel(hp_ref, hn_ref, prev_ref, cur_ref, next_ref, w_ref, o_ref, *, kind_blocks, q_scale):
    i, j = pl.program_id(0), pl.program_id(1)
    tb = cur_ref.shape[0]
    prev = prev_ref[...] * hp_ref[i].astype(F32)
    nxt = next_ref[...] * hn_ref[i].astype(F32)
    cur = cur_ref[...]
    ext = jnp.concatenate([prev, cur, nxt], axis=0)
    rows = ext.shape[0]
    half = DN_CONV // 2
    acc = cur * w_ref[half:half + 1, :]
    for tap in range(DN_CONV):
        if tap == half:
            continue
        shifted = pltpu.roll(ext, (half - tap) % rows, 0)[SUBLANES:SUBLANES + tb]
        acc = acc + shifted * w_ref[tap:tap + 1, :]
    y = _silu(acc)

    def l2(scale):
        for c in range(y.shape[1] // LANES):
            blk = y[:, c * LANES:(c + 1) * LANES]
            inv = lax.rsqrt(jnp.sum(blk * blk, axis=-1, keepdims=True) + EPS)
            o_ref[:, c * LANES:(c + 1) * LANES] = blk * (inv * scale)

    @pl.when(j < kind_blocks)
    def _():
        l2(q_scale)

    @pl.when(jnp.logical_and(j >= kind_blocks, j < 2 * kind_blocks))
    def _():
        l2(1.0)

    @pl.when(j >= 2 * kind_blocks)
    def _():
        o_ref[...] = y


def _short_conv(qkvz, conv_w, seq_lens, qk_width, dk):
    t = qkvz.shape[0]
    c = conv_w.shape[1]
    tb = seq_lens["tb"]
    cb = _pick(qk_width, 1024)
    nb8 = t // SUBLANES
    r8 = tb // SUBLANES
    grid_spec = pltpu.PrefetchScalarGridSpec(
        num_scalar_prefetch=2, grid=(t // tb, c // cb),
        in_specs=[pl.BlockSpec((SUBLANES, cb), lambda i, j, hp, hn: (jnp.maximum(i * r8 - 1, 0), j)),
                  pl.BlockSpec((tb, cb), lambda i, j, hp, hn: (i, j)),
                  pl.BlockSpec((SUBLANES, cb), lambda i, j, hp, hn: (jnp.minimum((i + 1) * r8, nb8 - 1), j)),
                  pl.BlockSpec((DN_CONV, cb), lambda i, j, hp, hn: (0, j))],
        out_specs=pl.BlockSpec((tb, cb), lambda i, j, hp, hn: (i, j)))
    return pl.pallas_call(
        functools.partial(_conv_kernel, kind_blocks=qk_width // cb, q_scale=dk ** -0.5),
        grid_spec=grid_spec,
        out_shape=jax.ShapeDtypeStruct((t, c), F32),
        compiler_params=_cparams("parallel", "parallel"), name="dn_conv",
    )(seq_lens["has_prev"], seq_lens["has_next"], qkvz, qkvz, qkvz, conv_w)


def _bdot(a, b):
    return _dot(a.astype(BF16), b.astype(BF16))


def _bdot_nt(a, b):
    return _dot_nt(a.astype(BF16), b.astype(BF16))


def _unit_tri_inverses(mats):
    n = mats[0].shape[0]
    row = lax.broadcasted_iota(jnp.int32, (n, 2 * n), 0)
    lane = lax.broadcasted_iota(jnp.int32, (n, 2 * n), 1)
    left = lane < n
    eye = (jnp.where(left, lane, lane - n) == row).astype(F32)

    def split(z):
        hi = z.astype(BF16)
        hi32 = hi.astype(F32)
        return hi, (z - hi32).astype(BF16), jnp.where(left, hi32, z - hi32).astype(BF16)

    def fold(r):
        r = r[:n] + r[n:]
        return r + pltpu.roll(r, n, 1)

    xs = [-a for a in mats]
    ts = [eye + x for x in xs]
    p = 1
    while True:
        square, update = 2 * p < n, p > 1
        new_xs, new_ts = [], []
        for x, t in zip(xs, ts):
            x_hi, x_lo, rhs = split(x)
            parts = [x_hi[:, :n], x_lo[:, :n]] if square else []
            if update:
                t_hi, t_lo, _ = split(t)
                parts += [t_hi[:, :n], t_lo[:, :n]]
            r = _dot(jnp.concatenate(parts, axis=0), rhs)
            new_xs.append(fold(r[:2 * n]) if square else None)
            new_ts.append(t + fold(r[-2 * n:]) if update else t)
        xs, ts = new_xs, new_ts
        if not square:
            return [t[:, :n] for t in ts]
        p *= 2


def _delta_kernel(blk_ref, seq_ref, first_ref, last_ref, q_ref, k_ref, v_ref, pc_ref, pr_ref, s0_ref, o_ref, sf_ref,
                  s_scr, *, hb, cg, reverse):
    n = pl.program_id(1)

    @pl.when(first_ref[n] == 1)
    def _():
        s_scr[...] = s0_ref[...]

    c = DN_CHUNK
    row = lax.broadcasted_iota(jnp.int32, (c, 2 * c), 0)
    col = lax.broadcasted_iota(jnp.int32, (c, 2 * c), 1)
    col = jnp.where(col < c, col, col - c)
    incl = (row <= col) if reverse else (row >= col)
    strict = (row < col) if reverse else (row > col)
    last = 0 if reverse else c - 1
    order = range(cg - 1, -1, -1) if reverse else range(cg)

    shared = {}
    for ci in order:
        rows = slice(ci * c, (ci + 1) * c)
        for kh in range(hb // 2):
            q = q_ref[rows, kh * LANES:(kh + 1) * LANES]
            k = k_ref[rows, kh * LANES:(kh + 1) * LANES]
            kb = k.astype(BF16)
            shared[ci, kh] = dict(q=q, k=k, kk=_dot_nt(kb, jnp.concatenate([kb, kb], axis=0)),
                                  qk=_dot_nt(q.astype(BF16), kb), kt=k.T)

    probs = [(ci, hh) for ci in order for hh in range(hb)]
    pre = {}
    for ci, hh in probs:
        pc = pc_ref[ci * c:(ci + 1) * c, :]
        pr = pr_ref[ci]
        beta = pc[:, hh:hh + 1]
        g_col = pc[:, hb + hh:hb + hh + 1]
        g_row2 = pr[hb + hh:hb + hh + 1, :]
        g_row = g_row2[:, :c]
        g_last = g_row[:, last:last + 1]
        decay2 = jnp.where(incl, jnp.exp(jnp.where(incl, g_col - g_row2, 0.0)), 0.0)
        pre[ci, hh] = dict(beta=beta, e_g=jnp.exp(g_col), decay=decay2[:, :c], gl=jnp.exp(g_last),
                           kscale=jnp.exp(g_last - g_row),
                           a=jnp.where(strict, beta * decay2 * shared[ci, hh // 2]["kk"], 0.0))
    tinvs = _unit_tri_inverses([pre[p]["a"] for p in probs])

    prep = {}
    for (ci, hh), tinv in zip(probs, tinvs):
        sh, pp = shared[ci, hh // 2], pre[ci, hh]
        v = v_ref[ci * c:(ci + 1) * c, hh * LANES:(hh + 1) * LANES]
        sol = _bdot(tinv, jnp.concatenate([pp["beta"] * v, (pp["beta"] * pp["e_g"]) * sh["k"]], axis=1))
        prep[ci, hh] = dict(
            u=sol[:, :LANES],
            wk_qg=jnp.concatenate([sol[:, LANES:], sh["q"] * pp["e_g"]], axis=0).astype(BF16),
            aqk=(pp["decay"] * sh["qk"]).astype(BF16),
            kdt=(sh["kt"] * pp["kscale"]).astype(BF16),
            gl=pp["gl"])

    states = [s_scr[hh] for hh in range(hb)]
    for ci in order:
        boths = [_dot(prep[ci, hh]["wk_qg"], states[hh].astype(BF16)) for hh in range(hb)]
        ws = [(prep[ci, hh]["u"] - boths[hh][:c]).astype(BF16) for hh in range(hb)]
        for hh in range(hb):
            o_ref[ci * c:(ci + 1) * c, hh * LANES:(hh + 1) * LANES] = (
                boths[hh][c:] + _dot(prep[ci, hh]["aqk"], ws[hh]))
        states = [prep[ci, hh]["gl"] * states[hh] + _dot(prep[ci, hh]["kdt"], ws[hh]) for hh in range(hb)]
    for hh in range(hb):
        s_scr[hh] = states[hh]

    @pl.when(last_ref[n] == 1)
    def _():
        sf_ref[...] = s_scr[...]


def _delta_scan(qkvc, p_col, p_row, s0, seq_lens, *, hk, hv, dk, dv, direction, hb, cg):
    assert dk == LANES and dv == LANES and hv == 2 * hk
    c = DN_CHUNK * cg
    hg = hv // hb
    reverse = direction == 1
    blk, seq, first, last = [], [], [], []
    start = 0
    for s, length in enumerate(seq_lens):
        nb = length // c
        ids = list(range(start, start + nb))
        blk += ids[::-1] if reverse else ids
        seq += [s] * nb
        first += [1] + [0] * (nb - 1)
        last += [0] * (nb - 1) + [1]
        start += nb
    tables = [jnp.asarray(np.asarray(a, np.int32)) for a in (blk, seq, first, last)]
    t = start * c

    qw = (hb // 2) * dk
    vw = hb * dv
    state_spec = pl.BlockSpec((None, hb, dk, dv), lambda g, n, bt, st, ft, lt: (st[n], g, 0, 0))
    grid_spec = pltpu.PrefetchScalarGridSpec(
        num_scalar_prefetch=4, grid=(hg, len(blk)),
        in_specs=[pl.BlockSpec((c, qw), lambda g, n, bt, st, ft, lt: (bt[n], g)),
                  pl.BlockSpec((c, qw), lambda g, n, bt, st, ft, lt: (bt[n], hk * dk // qw + g)),
                  pl.BlockSpec((c, vw), lambda g, n, bt, st, ft, lt: (bt[n], 2 * hk * dk // vw + g)),
                  pl.BlockSpec((None, None, c, 2 * hb), lambda g, n, bt, st, ft, lt: (g, direction, bt[n], 0)),
                  pl.BlockSpec((None, None, cg, 2 * hb, 2 * DN_CHUNK),
                               lambda g, n, bt, st, ft, lt: (g, direction, bt[n], 0, 0)),
                  state_spec],
        out_specs=[pl.BlockSpec((c, vw), lambda g, n, bt, st, ft, lt: (bt[n], g)), state_spec],
        scratch_shapes=[pltpu.VMEM((hb, dk, dv), F32)])
    return pl.pallas_call(
        functools.partial(_delta_kernel, hb=hb, cg=cg, reverse=reverse),
        grid_spec=grid_spec,
        out_shape=[jax.ShapeDtypeStruct((t, hv * dv), F32), jax.ShapeDtypeStruct(s0.shape, F32)],
        compiler_params=_cparams("parallel", "arbitrary"), name="dn_scan",
    )(*tables, qkvc, qkvc, qkvc, p_col, p_row, s0)


def _gated_norm_kernel(of_ref, ob_ref, z_ref, g_ref, o_ref):
    for c in range(o_ref.shape[1] // LANES):
        sl = slice(c * LANES, (c + 1) * LANES)
        o = of_ref[:, sl] + ob_ref[:, sl]
        o = o * lax.rsqrt(jnp.mean(o * o, axis=-1, keepdims=True) + EPS) * g_ref[...]
        o_ref[:, sl] = (o * _silu(z_ref[:, sl])).astype(o_ref.dtype)


def _gated_norm(o_f, o_b, qkvz, z_col0, norm_g):
    t, w = o_f.shape
    tb = _pick(t, 256)
    cb = _pick(math.gcd(w, z_col0), 1024)
    tok = pl.BlockSpec((tb, cb), lambda i, j: (i, j))
    return pl.pallas_call(
        _gated_norm_kernel, grid=(t // tb, w // cb),
        in_specs=[tok, tok, pl.BlockSpec((tb, cb), lambda i, j: (i, z_col0 // cb + j)),
                  pl.BlockSpec((1, LANES), lambda i, j: (0, 0))],
        out_specs=tok, out_shape=jax.ShapeDtypeStruct((t, w), BF16),
        compiler_params=_cparams("parallel", "parallel"), name="dn_gated_norm",
    )(o_f, o_b, qkvz, norm_g.reshape(1, LANES).astype(F32))


def _na_ctx_kernel(q_ref, k_ref, v_ref, o_ref, *, scale):
    q = (q_ref[...] * scale).astype(BF16)
    s = _dot_nt(q, k_ref[...].astype(BF16))
    p = jnp.exp(s - jnp.max(s, axis=-1, keepdims=True))
    o = _dot(p.astype(BF16), v_ref[...].astype(BF16)) / jnp.sum(p, axis=-1, keepdims=True)
    o_ref[...] = o.astype(o_ref.dtype)


def _na_context(qkv, n_seq, seq_len, heads):
    blk = lambda off: pl.BlockSpec((seq_len, LANES), lambda b, h: (b, off * heads + h))
    return pl.pallas_call(
        functools.partial(_na_ctx_kernel, scale=LANES ** -0.5),
        grid=(n_seq, heads), in_specs=[blk(0), blk(1), blk(2)],
        out_specs=pl.BlockSpec((seq_len, LANES), lambda b, h: (b, h)),
        out_shape=jax.ShapeDtypeStruct((n_seq * seq_len, heads * LANES), BF16),
        compiler_params=_cparams("parallel", "parallel"), name="na_ctx",
    )(qkv, qkv, qkv)


def _na_lat_kernel(q_ref, k_ref, v_ref, kc_ref, vc_ref, bias_ref, o_ref, kb_scr, vb_scr, *, rows, wr, scale):
    kb_scr[...] = k_ref[...].astype(BF16)
    vb_scr[...] = v_ref[...].astype(BF16)
    kc = kc_ref[...].astype(BF16)
    vc = vc_ref[...].astype(BF16)
    w = GRID_W

    group = math.gcd(rows, NA_ROWS_PER_STEP)

    def body(it, carry):
        rs = [it * group + k for k in range(group)]
        r0s = [jnp.clip(r - wr // 2, 0, rows - wr) for r in rs]
        qs = [(q_ref[pl.ds(pl.multiple_of(r * w, w), w), :] * scale).astype(BF16) for r in rs]
        wins = [pl.ds(pl.multiple_of(r0 * w, w), wr * w) for r0 in r0s]
        s_lats = [_dot_nt(q, kb_scr[win, :]) for q, win in zip(qs, wins)]
        s_ctxs = [_dot_nt(q, kc) for q in qs]
        outs = []
        for r, r0, win, s_lat, s_ctx in zip(rs, r0s, wins, s_lats, s_ctxs):
            d0 = r0 - r + WIN_R - 1
            s_lat = s_lat + jnp.concatenate([bias_ref[d0 + 2 * p] for p in range(wr // 2)], axis=1)
            m = jnp.maximum(jnp.max(s_lat, axis=-1, keepdims=True), jnp.max(s_ctx, axis=-1, keepdims=True))
            p_lat = jnp.exp(s_lat - m)
            p_ctx = jnp.exp(s_ctx - m)
            den = jnp.sum(p_lat, axis=-1, keepdims=True) + jnp.sum(p_ctx, axis=-1, keepdims=True)
            outs.append((p_lat.astype(BF16), p_ctx.astype(BF16), den))
        for r, win, (p_lat, p_ctx, den) in zip(rs, wins, outs):
            o = _dot(p_lat, vb_scr[win, :]) + _dot(p_ctx, vc)
            o_ref[pl.ds(pl.multiple_of(r * w, w), w), :] = (o / den).astype(o_ref.dtype)
        return carry

    lax.fori_loop(0, rows // group, body, 0)


def _na_bias_pairs(rpb):
    col = np.arange(GRID_W)
    c0 = np.clip(col - WIN_C // 2, 0, GRID_W - WIN_C)
    col_ok = (col[None, :] >= c0[:, None]) & (col[None, :] < c0[:, None] + WIN_C)
    dc_idx = np.clip(col[None, :] - col[:, None], 1 - WIN_C, WIN_C - 1) + WIN_C - 1
    bias = jnp.where(col_ok[None, None], rpb.astype(F32)[:, :, dc_idx], NEG)
    return jnp.concatenate([bias[:, :-1], bias[:, 1:]], axis=-1)


def _na_latent(qkv, cache_k, cache_v, rpb, row0, n_seq, seq_len, heads):
    rows = seq_len // GRID_W
    wr = min(WIN_R, rows)
    assert wr % 2 == 0 and row0 % seq_len == 0
    b0 = row0 // seq_len
    past = cache_k.shape[1]
    bias = _na_bias_pairs(rpb)
    blk = lambda off: pl.BlockSpec((seq_len, LANES), lambda b, h: (b0 + b, off * heads + h))
    cache = pl.BlockSpec((None, past, LANES), lambda b, h: (b, 0, h))
    return pl.pallas_call(
        functools.partial(_na_lat_kernel, rows=rows, wr=wr, scale=LANES ** -0.5),
        grid=(n_seq, heads),
        in_specs=[blk(0), blk(1), blk(2), cache, cache,
                  pl.BlockSpec((None,) + bias.shape[1:], lambda b, h: (h, 0, 0, 0))],
        out_specs=pl.BlockSpec((seq_len, LANES), lambda b, h: (b, h)),
        out_shape=jax.ShapeDtypeStruct((n_seq * seq_len, heads * LANES), BF16),
        scratch_shapes=[pltpu.VMEM((seq_len, LANES), BF16), pltpu.VMEM((seq_len, LANES), BF16)],
        compiler_params=_cparams("parallel", "parallel"), name="na_lat",
    )(qkv, qkv, qkv, cache_k, cache_v, bias)


def _top16(s, row):
    n = s.shape[0]
    vals = []
    rank = jnp.full(s.shape, PEER_TOPK, jnp.int32)
    for r in range(PEER_TOPK):
        m = jnp.max(s, axis=0, keepdims=True)
        idx = jnp.min(jnp.where(s == m, row, n), axis=0, keepdims=True)
        hit = row == idx
        vals.append(m)
        rank = jnp.where(hit, r, rank)
        s = jnp.where(hit, -jnp.inf, s)
    return jnp.concatenate(vals, axis=0), rank


def _peer_topk_kernel(q_ref, keys_ref, n_ref, e1_ref, b_ref, e2_ref, *, heads, half):
    tb = q_ref.shape[0]
    nk = keys_ref.shape[2]
    kk = PEER_TOPK
    assert kk == 2 * SUBLANES
    row = lax.broadcasted_iota(jnp.int32, (nk, tb), 0)
    row_k = lax.broadcasted_iota(jnp.int32, (kk, tb), 0)
    sub = lax.broadcasted_iota(jnp.int32, (SUBLANES, tb), 0)
    flat = jnp.concatenate([row_k] + [a * kk + sub for a in range(1, SUBLANES)]
                           + [(SUBLANES + g * SUBLANES + sub) * kk for g in range(kk // SUBLANES - 1)], axis=0)
    for h in range(heads):
        q1 = q_ref[:, (2 * h) * half:(2 * h + 1) * half]
        q2 = q_ref[:, (2 * h + 1) * half:(2 * h + 2) * half]
        s1 = _dot_nt(keys_ref[h, 0], q1, HI)
        s2 = _dot_nt(keys_ref[h, 1], q2, HI)
        v1, rank1 = _top16(s1, row)
        v2, rank2 = _top16(s2, row)
        cand = jnp.concatenate([v1[0:1, :] + v2] + [v1[a:a + 1, :] + v2[:SUBLANES] for a in range(1, SUBLANES)]
                               + [v1[SUBLANES:, :] + v2[0:1, :]], axis=0)
        cnt = jnp.zeros((kk, tb), jnp.int32)
        zsum = jnp.zeros((1, tb), F32)
        best0 = None
        for r in range(kk):
            m = jnp.max(cand, axis=0, keepdims=True)
            idx = jnp.min(jnp.where(cand == m, flat, kk * kk), axis=0, keepdims=True)
            best0 = m if r == 0 else best0
            zsum = zsum + jnp.exp(m - best0)
            cnt = cnt + (row_k == idx // kk).astype(jnp.int32)
            cand = jnp.where(flat == idx, -jnp.inf, cand)
        n_i = jnp.zeros((nk, tb), jnp.int32)
        for a in range(kk):
            n_i = jnp.where(rank1 == a, cnt[a:a + 1, :], n_i)
        n_ref[h] = n_i.astype(F32)
        e1_ref[h] = jnp.exp(s1 - v1[0:1, :]) / zsum
        b_ref[h] = pltpu.bitcast(rank2.astype(F32).astype(BF16), jnp.uint32)
        e2_ref[h] = pltpu.bitcast(jnp.exp(s2 - v2[0:1, :]).astype(BF16), jnp.uint32)


def _peer_topk(q, keys):
    t = q.shape[0]
    heads, _, nk, half = keys.shape
    tb = _pick(t, 256)
    out = [jax.ShapeDtypeStruct((heads, nk, t), F32)] * 2 + [jax.ShapeDtypeStruct((heads, nk // 2, t), jnp.uint32)] * 2
    spec = pl.BlockSpec((heads, nk, tb), lambda i: (0, 0, i))
    pspec = pl.BlockSpec((heads, nk // 2, tb), lambda i: (0, 0, i))
    return pl.pallas_call(
        functools.partial(_peer_topk_kernel, heads=heads, half=half),
        grid=(t // tb,),
        in_specs=[pl.BlockSpec((tb, q.shape[1]), lambda i: (i, 0)),
                  pl.BlockSpec(keys.shape, lambda i: (0, 0, 0, 0))],
        out_specs=[spec, spec, pspec, pspec], out_shape=out,
        compiler_params=_cparams("parallel"), name="peer_topk",
    )(q, keys)


def _gelu(x):
    return 0.5 * x * (1.0 + lax.erf(x * (2.0 ** -0.5)))


def _peer_dense_kernel(xt_ref, u_ref, vt_ref, n_ref, e1_ref, b_ref, e2_ref, o_ref, acc_scr, a_scr, w_scr,
                       cnt_scr, e1_scr, *, heads, ib, tw):
    j = pl.program_id(1)

    @pl.when(j == 0)
    def _():
        acc_scr[...] = jnp.zeros(acc_scr.shape, F32)

    nk = 2 * b_ref.shape[1]
    tb = xt_ref.shape[1]
    n_slab = tb // tw

    def scores(s):
        a_scr[s] = _dot(u_ref[...], xt_ref[:, s * tw:(s + 1) * tw])

    for h in range(heads):
        for ii in range(ib):
            cnt_scr[h, ii] = jnp.broadcast_to(n_ref[h, ii:ii + 1, :], (BF16_ROWS, tb)).astype(BF16)
            e1_scr[h, ii] = jnp.broadcast_to(e1_ref[h, ii:ii + 1, :], (BF16_ROWS, tb)).astype(BF16)

    def gate_tiles(s):
        rep = nk // BF16_ROWS
        for ii in range(ib):
            rows = slice(ii * nk, (ii + 1) * nk)
            for tc in range(tw // LANES):
                lanes = slice(s * tw + tc * LANES, s * tw + (tc + 1) * LANES)
                g = jnp.zeros((rep, BF16_ROWS, LANES), BF16)
                for h in range(heads):
                    cnt = cnt_scr[h, ii, :, lanes][None]
                    e1 = e1_scr[h, ii, :, lanes][None]
                    b = pltpu.bitcast(b_ref[h, :, lanes], BF16).reshape(g.shape)
                    e2 = pltpu.bitcast(e2_ref[h, :, lanes], BF16).reshape(g.shape)
                    g = g + jnp.where(b < cnt, e1 * e2, jnp.zeros((), BF16))
                a = a_scr[s, rows, tc * LANES:(tc + 1) * LANES]
                w_scr[s, rows, tc * LANES:(tc + 1) * LANES] = g.reshape(nk, LANES) * _gelu(a).astype(BF16)

    scores(0)
    for s in range(n_slab):
        if s + 1 < n_slab:
            scores(s + 1)
        gate_tiles(s)
        acc_scr[:, s * tw:(s + 1) * tw] += _dot(vt_ref[...], w_scr[s])

    @pl.when(j == pl.num_programs(1) - 1)
    def _():
        o_ref[...] = acc_scr[...].T


def _peer_dense(ht, u_tab, v_tab, sel):
    d, t = ht.shape
    ne = u_tab.shape[0]
    n_i, e1, b_j, e2 = sel
    heads, nk, _ = n_i.shape
    tb = _pick(t, 512)
    tw = _pick(tb, MXU_WIDTH)
    ib = SUBLANES
    eb = ib * nk
    u_tab = u_tab.astype(BF16)
    vt_tab = v_tab.reshape(ne // eb, eb, d).transpose(0, 2, 1).astype(BF16)
    n_i = n_i.reshape(heads, nk // ib, ib, t)
    e1 = e1.reshape(heads, nk // ib, ib, t)
    row_spec = pl.BlockSpec((heads, None, ib, tb), lambda i, j: (0, j, 0, i))
    col_spec = pl.BlockSpec((heads, nk // 2, tb), lambda i, j: (0, 0, i))
    return pl.pallas_call(
        functools.partial(_peer_dense_kernel, heads=heads, ib=ib, tw=tw),
        grid=(t // tb, ne // eb),
        in_specs=[pl.BlockSpec((d, tb), lambda i, j: (0, i)),
                  pl.BlockSpec((eb, d), lambda i, j: (j, 0)),
                  pl.BlockSpec((None, d, eb), lambda i, j: (j, 0, 0)),
                  row_spec, row_spec, col_spec, col_spec],
        out_specs=pl.BlockSpec((tb, d), lambda i, j: (i, 0)),
        out_shape=jax.ShapeDtypeStruct((t, d), F32),
        scratch_shapes=[pltpu.VMEM((d, tb), F32), pltpu.VMEM((tb // tw, eb, tw), F32),
                        pltpu.VMEM((tb // tw, eb, tw), BF16),
                        pltpu.VMEM((heads, ib, BF16_ROWS, tb), BF16), pltpu.VMEM((heads, ib, BF16_ROWS, tb), BF16)],
        compiler_params=_cparams("parallel", "arbitrary"), name="peer_dense",
    )(ht, u_tab, vt_tab, n_i, e1, b_j, e2)


def _peer(h, ht, w_q, keys, u_tab, v_tab):
    q = _matmul(h, w_q.astype(BF16))
    sel = _peer_topk(q, keys.astype(F32))
    return _peer_dense(ht, u_tab, v_tab, sel)


def _seq_flags(groups, n_ctx_seq, tb):
    ctx_len = groups.ctx_tokens // n_ctx_seq
    starts = np.concatenate([np.arange(n_ctx_seq) * ctx_len,
                             groups.ctx_tokens + np.arange(groups.n_lat) * groups.lat_len])
    ends = np.concatenate([starts[1:], [groups.total]])
    blk = np.arange(groups.total // tb) * tb
    has_prev = ~np.isin(blk, starts)
    has_next = ~np.isin(blk + tb, ends)
    return {"tb": tb, "has_prev": jnp.asarray(has_prev, jnp.int32), "has_next": jnp.asarray(has_next, jnp.int32)}


def _deltanet(h, x, gate, groups, n_ctx_seq, w_in, conv_w, a_log, dt_bias, norm_g, w_o, state_delta):
    hv = a_log.shape[-1]
    dk, dv = state_delta.shape[-2:]
    qkv_w = conv_w.shape[1]
    v_w = hv * dv
    hk = (qkv_w - v_w) // (2 * dk)
    ctx_len = groups.ctx_tokens // n_ctx_seq
    t = groups.total
    c = DN_CHUNK

    qkvz = _matmul(h, w_in.astype(BF16), n_cols=qkv_w + v_w)
    p = _gates(h, w_in[:, qkv_w + v_w:].astype(F32), a_log, dt_bias)
    tb = _pick(math.gcd(ctx_len, groups.lat_len), 256)
    qkvc = _short_conv(qkvz, conv_w.astype(F32), _seq_flags(groups, n_ctx_seq, tb), hk * dk, dk)

    hb = min(DN_HEADS_PER_STEP, hv)
    cg = DN_CHUNKS_PER_STEP
    hg = hv // hb
    p5 = p.reshape(t, 2, 2, hg, hb).transpose(3, 2, 0, 1, 4).reshape(hg, 2, t, 2 * hb)
    p_row = p5.reshape(hg, 2, t // c, c, 2 * hb).transpose(0, 1, 2, 4, 3)
    p_row = jnp.concatenate([p_row, p_row], axis=-1)
    seq_lens = [ctx_len] * n_ctx_seq + [groups.lat_len] * groups.n_lat
    zero_state = jnp.zeros((n_ctx_seq, hv, dk, dv), F32)
    outs, states = [], []
    for direction in range(2):
        s0 = jnp.concatenate([zero_state, state_delta[:, direction].astype(F32)], axis=0)
        o, s_fin = _delta_scan(qkvc, p5, p_row, s0, seq_lens, hk=hk, hv=hv, dk=dk, dv=dv, direction=direction,
                               hb=hb, cg=cg)
        outs.append(o)
        states.append(s_fin[:n_ctx_seq])
    og = _gated_norm(outs[0], outs[1], qkvz, qkv_w, norm_g)
    x = _matmul(og, w_o.astype(BF16), groups=groups, res=x, gate=gate)
    return x, jnp.stack(states, axis=1)


def _natten(h, x, gate, groups, n_ctx_seq, w_qkv, rpb, w_o, cache_k, cache_v):
    d = h.shape[1]
    heads = d // LANES
    ctx_len = groups.ctx_tokens // n_ctx_seq
    qkv = _matmul(h, w_qkv.astype(BF16))
    o_c = _na_context(qkv, n_ctx_seq, ctx_len, heads)
    n_lat, past = cache_k.shape[:2]
    o_l = _na_latent(qkv, cache_k.reshape(n_lat, past, d), cache_v.reshape(n_lat, past, d), rpb,
                     groups.ctx_tokens, groups.n_lat, groups.lat_len, heads)
    x = _matmul(jnp.concatenate([o_c, o_l], axis=0), w_o.astype(BF16), groups=groups, res=x, gate=gate)
    k_c = qkv[:groups.ctx_tokens, d:2 * d].reshape(n_ctx_seq, ctx_len, heads, LANES)
    v_c = qkv[:groups.ctx_tokens, 2 * d:].reshape(n_ctx_seq, ctx_len, heads, LANES)
    return x, k_c, v_c


def kernel(x_prompt, x_sample, c, state_delta, cache_k, cache_v, c_ctx, ada_w, ada_b, norm1_g, norm2_g, final_g,
           dn_w_in, dn_conv_w, dn_a_log, dn_dt_bias, dn_norm_g, dn_w_o, na_w_qkv, na_rpb, na_w_o,
           peer_w_q, peer_keys, peer_u, peer_v):
    n_ctx_seq, ctx_len, d = x_prompt.shape
    n_lat, lat_len, _ = x_sample.shape
    depth = ada_w.shape[0]
    groups = _Groups(n_ctx_seq * ctx_len, n_lat, lat_len)
    n_grp = 1 + n_lat

    x = jnp.concatenate([x_prompt.reshape(-1, d), x_sample.reshape(-1, d)], axis=0).astype(F32)
    pad = (-n_grp) % SUBLANES
    cvecs = jnp.concatenate([c_ctx[None], c, jnp.zeros((pad, d), c.dtype)], axis=0).astype(F32)
    mods = _adaln(cvecs, ada_w.astype(F32), ada_b.astype(F32))[:, :n_grp]
    mods = mods.reshape(depth, n_grp, 6, 1, d).transpose(0, 2, 1, 3, 4)

    states, ctx_k, ctx_v = [], [], []
    delta = gate = None
    for i in range(depth):
        sh1, sc1, g1, sh2, sc2, g2 = (mods[i, k] for k in range(6))
        j = i // 2
        x, h = _normmod(x, norm1_g[i].astype(F32), groups, delta, gate, sh1, sc1)
        if i % 2 == 0:
            x, s_fin = _deltanet(h, x, g1, groups, n_ctx_seq, dn_w_in[j], dn_conv_w[j], dn_a_log[j], dn_dt_bias[j],
                                 dn_norm_g[j], dn_w_o[j], state_delta[:, j])
            states.append(s_fin)
        else:
            x, k_c, v_c = _natten(h, x, g1, groups, n_ctx_seq, na_w_qkv[j], na_rpb[j], na_w_o[j],
                                  cache_k[:, j], cache_v[:, j])
            ctx_k.append(k_c)
            ctx_v.append(v_c)
        x, h, ht = _normmod(x, norm2_g[i].astype(F32), groups, None, None, sh2, sc2, transposed=True)
        delta, gate = _peer(h, ht, peer_w_q[i], peer_keys[i], peer_u[i], peer_v[i]), g2
    final = functools.partial(_normmod, x, final_g.astype(F32), groups, delta, gate, out_dtype=F32)
    y_prompt = final(rows=(0, groups.ctx_tokens)).reshape(x_prompt.shape)
    y_sample = final(rows=(groups.ctx_tokens, n_lat * lat_len)).reshape(x_sample.shape)
    return (y_prompt, y_sample, jnp.stack(states, axis=1), jnp.stack(ctx_k, axis=1), jnp.stack(ctx_v, axis=1))
```

```python
import functools
import math

import jax
import jax.numpy as jnp
import numpy as np
from jax import lax
from jax.experimental import pallas as pl
from jax.experimental.pallas import tpu as pltpu

GRID_W = 64
WIN_R = 8
WIN_C = 16
DN_CHUNK = 64
DN_CONV = 5
PEER_TOPK = 16
EPS = 1e-6
DN_HEADS_PER_STEP = 8
DN_CHUNKS_PER_STEP = 2
NA_ROWS_PER_STEP = 4
CONV_ROWS = 64
TABLE_PARTS = 2

LANES = 128
SUBLANES = 8
BF16_ROWS = 16
MXU_WIDTH = 256
VMEM_LIMIT = 56 * 1024 * 1024

F32 = jnp.float32
BF16 = jnp.bfloat16
HI = lax.Precision.HIGHEST
NEG = -1e30


def _cparams(*sem):
    return pltpu.CompilerParams(dimension_semantics=sem, vmem_limit_bytes=VMEM_LIMIT)


def _dot(a, b, precision=None):
    return jnp.dot(a, b, preferred_element_type=F32, precision=precision)


def _dot_nt(a, b, precision=None):
    return lax.dot_general(a, b, (((1,), (1,)), ((), ())), preferred_element_type=F32, precision=precision)


def _silu(x):
    return x * (1.0 / (1.0 + jnp.exp(-x)))


def _pick(n, pref):
    b = min(n, pref)
    while n % b:
        b -= 1
    return b


def _adaln_kernel(c_ref, w_ref, b_ref, o_ref):
    o_ref[...] = _dot(_silu(c_ref[...]), w_ref[...], HI) + b_ref[...]


def _adaln(cvecs, ada_w, ada_b):
    depth, d, n6 = ada_w.shape
    r = cvecs.shape[0]
    tn = _pick(n6, 1024)
    return pl.pallas_call(
        _adaln_kernel,
        grid=(depth, n6 // tn),
        in_specs=[pl.BlockSpec((r, d), lambda i, j: (0, 0)),
                  pl.BlockSpec((None, d, tn), lambda i, j: (i, 0, j)),
                  pl.BlockSpec((None, 1, tn), lambda i, j: (i, 0, j))],
        out_specs=pl.BlockSpec((None, r, tn), lambda i, j: (i, 0, j)),
        out_shape=jax.ShapeDtypeStruct((depth, r, n6), F32),
        compiler_params=_cparams("parallel", "parallel"),
        name="adaln",
    )(cvecs, ada_w, ada_b.reshape(depth, 1, n6))


def _normmod_kernel(*refs, has_delta, has_mod, emit_x, emit_t):
    it = iter(refs)
    x_ref = next(it)
    if has_delta:
        d_ref, gate_ref = next(it), next(it)
    g_ref = next(it)
    if has_mod:
        sh_ref, sc_ref = next(it), next(it)
    if emit_x:
        xo_ref = next(it)
    h_ref = next(it)
    x = x_ref[...]
    if has_delta:
        x = x + gate_ref[...] * d_ref[...]
    if emit_x:
        xo_ref[...] = x
    y = x * lax.rsqrt(jnp.mean(x * x, axis=-1, keepdims=True) + EPS) * g_ref[...]
    if has_mod:
        y = y * (1.0 + sc_ref[...]) + sh_ref[...]
    h_ref[...] = y.astype(h_ref.dtype)
    if emit_t:
        next(it)[...] = y.T.astype(h_ref.dtype)


def _normmod(x, gain, groups, delta=None, gate=None, shift=None, scale=None, out_dtype=BF16, transposed=False,
             rows=None):
    t, d = x.shape
    tb = groups.block(256)
    row0, t_out = (0, t) if rows is None else rows
    b0 = row0 // tb
    has_delta, has_mod = delta is not None, shift is not None
    emit_x = has_delta and rows is None
    tok = pl.BlockSpec((tb, d), lambda i: (b0 + i, 0))
    out_tok = pl.BlockSpec((tb, d), lambda i: (i, 0))
    grp = pl.BlockSpec((None, 1, d), lambda i: (groups.of_block(b0 + i, tb), 0, 0))
    args, specs = [x], [tok]
    if has_delta:
        args += [delta, gate]
        specs += [tok, grp]
    args.append(gain.reshape(1, d))
    specs.append(pl.BlockSpec((1, d), lambda i: (0, 0)))
    if has_mod:
        args += [shift, scale]
        specs += [grp, grp]
    out_shape, out_specs = [], []
    if emit_x:
        out_shape.append(jax.ShapeDtypeStruct((t, d), F32))
        out_specs.append(out_tok)
    out_shape.append(jax.ShapeDtypeStruct((t_out, d), out_dtype))
    out_specs.append(out_tok)
    if transposed:
        out_shape.append(jax.ShapeDtypeStruct((d, t_out), out_dtype))
        out_specs.append(pl.BlockSpec((d, tb), lambda i: (0, i)))
    res = pl.pallas_call(
        functools.partial(_normmod_kernel, has_delta=has_delta, has_mod=has_mod, emit_x=emit_x,
                          emit_t=transposed),
        grid=(t_out // tb,), in_specs=specs, out_specs=out_specs, out_shape=out_shape,
        compiler_params=_cparams("parallel"), name="normmod",
    )(*args)
    if rows is not None:
        return res[0]
    res = list(res) if has_delta else [x] + list(res)
    return tuple(res)


class _Groups:
    def __init__(self, ctx_tokens, n_lat, lat_len):
        self.ctx_tokens, self.n_lat, self.lat_len = ctx_tokens, n_lat, lat_len
        self.total = ctx_tokens + n_lat * lat_len

    def block(self, pref):
        return _pick(math.gcd(self.ctx_tokens, self.lat_len), pref)

    def of_block(self, i, tb):
        nctx = self.ctx_tokens // tb
        return jnp.where(i < nctx, 0, 1 + (i - nctx) // (self.lat_len // tb))


def _matmul_kernel(*refs, has_res):
    if has_res:
        a_ref, w_ref, r_ref, g_ref, o_ref = refs
        o_ref[...] = r_ref[...] + g_ref[...] * _dot(a_ref[...], w_ref[...])
    else:
        a_ref, w_ref, o_ref = refs
        o_ref[...] = _dot(a_ref[...], w_ref[...]).astype(o_ref.dtype)


def _matmul(a, w, groups=None, res=None, gate=None, out_dtype=F32, n_cols=None, col0=0):
    m, k = a.shape
    n = w.shape[1] if n_cols is None else n_cols
    tm = _pick(m, 1024) if groups is None else groups.block(1024)
    tn = _pick(math.gcd(n, col0) if col0 else n, 512)
    cb0 = col0 // tn
    has_res = res is not None
    args = [a, w]
    specs = [pl.BlockSpec((tm, k), lambda i, j: (i, 0)), pl.BlockSpec((k, tn), lambda i, j: (0, cb0 + j))]
    if has_res:
        args += [res, gate]
        specs += [pl.BlockSpec((tm, tn), lambda i, j: (i, j)),
                  pl.BlockSpec((None, 1, tn), lambda i, j: (groups.of_block(i, tm), 0, j))]
    return pl.pallas_call(
        functools.partial(_matmul_kernel, has_res=has_res),
        grid=(m // tm, n // tn), in_specs=specs,
        out_specs=pl.BlockSpec((tm, tn), lambda i, j: (i, j)),
        out_shape=jax.ShapeDtypeStruct((m, n), out_dtype),
        compiler_params=_cparams("parallel", "parallel"), name="matmul",
    )(*args)


def _gates_kernel(h_ref, w_ref, alog_ref, dtb_ref, p_ref, *, hv, n_chunks):
    lane = lax.broadcasted_iota(jnp.int32, (DN_CHUNK, 4 * hv), 1)
    row = lax.broadcasted_iota(jnp.int32, (DN_CHUNK, DN_CHUNK), 0)
    col = lax.broadcasted_iota(jnp.int32, (DN_CHUNK, DN_CHUNK), 1)
    tri_lo = (row >= col).astype(F32)
    tri_up = (row <= col).astype(F32)
    bwd = ((lane // hv) % 2) == 1
    for c in range(n_chunks):
        sl = slice(c * DN_CHUNK, (c + 1) * DN_CHUNK)
        ba = _dot(h_ref[sl, :].astype(F32), w_ref[...], HI)
        beta = 1.0 / (1.0 + jnp.exp(-ba))
        z = ba + dtb_ref[...]
        softplus = jnp.maximum(z, 0.0) + jnp.log(1.0 + jnp.exp(-jnp.abs(z)))
        g = -jnp.exp(alog_ref[...]) * softplus
        cum = jnp.where(bwd, _dot(tri_up, g, HI), _dot(tri_lo, g, HI))
        p_ref[sl, :] = jnp.where(lane < 2 * hv, beta, cum)


def _gates(h, w_ba, a_log, dt_bias):
    t, d = h.shape
    hv = a_log.shape[-1]
    tb = _pick(t, 256)
    zeros = jnp.zeros((1, 2 * hv), F32)
    alog_row = jnp.concatenate([zeros, a_log.reshape(1, 2 * hv).astype(F32)], axis=1)
    dtb_row = jnp.concatenate([zeros, dt_bias.reshape(1, 2 * hv).astype(F32)], axis=1)
    return pl.pallas_call(
        functools.partial(_gates_kernel, hv=hv, n_chunks=tb // DN_CHUNK),
        grid=(t // tb,),
        in_specs=[pl.BlockSpec((tb, d), lambda i: (i, 0)),
                  pl.BlockSpec((d, 4 * hv), lambda i: (0, 0)),
                  pl.BlockSpec((1, 4 * hv), lambda i: (0, 0)),
                  pl.BlockSpec((1, 4 * hv), lambda i: (0, 0))],
        out_specs=pl.BlockSpec((tb, 4 * hv), lambda i: (i, 0)),
        out_shape=jax.ShapeDtypeStruct((t, 4 * hv), F32),
        compiler_params=_cparams("parallel"), name="dn_gates",
    )(h, w_ba, alog_row, dtb_row)


def _conv_kernel(hp_ref, hn_ref, prev_ref, cur_ref, next_ref, w_ref, o_ref, ext_scr, *, kind_blocks, q_scale):
    i, j = pl.program_id(0), pl.program_id(1)
    tb, cb = cur_ref.shape
    half = DN_CONV // 2
    ext_scr[:SUBLANES] = prev_ref[...] * hp_ref[i].astype(F32)
    ext_scr[SUBLANES:SUBLANES + tb] = cur_ref[...]
    ext_scr[SUBLANES + tb:] = next_ref[...] * hn_ref[i].astype(F32)
    rc = math.gcd(tb, CONV_ROWS)

    def tiles(finish):
        for c in range(cb // LANES):
            lanes = slice(c * LANES, (c + 1) * LANES)
            for r in range(tb // rc):
                acc = None
                for tap in range(DN_CONV):
                    start = SUBLANES + r * rc + tap - half
                    term = ext_scr[start:start + rc, lanes] * w_ref[tap:tap + 1, lanes]
                    acc = term if acc is None else acc + term
                o_ref[r * rc:(r + 1) * rc, lanes] = finish(_silu(acc))

    def l2(scale):
        return lambda y: y * (lax.rsqrt(jnp.sum(y * y, axis=-1, keepdims=True) + EPS) * scale)

    @pl.when(j < kind_blocks)
    def _():
        tiles(l2(q_scale))

    @pl.when(jnp.logical_and(j >= kind_blocks, j < 2 * kind_blocks))
    def _():
        tiles(l2(1.0))

    @pl.when(j >= 2 * kind_blocks)
    def _():
        tiles(lambda y: y)


def _short_conv(qkvz, conv_w, seq_lens, qk_width, dk):
    t = qkvz.shape[0]
    c = conv_w.shape[1]
    tb = seq_lens["tb"]
    cb = _pick(qk_width, 1024)
    nb8 = t // SUBLANES
    r8 = tb // SUBLANES
    grid_spec = pltpu.PrefetchScalarGridSpec(
        num_scalar_prefetch=2, grid=(t // tb, c // cb),
        in_specs=[pl.BlockSpec((SUBLANES, cb), lambda i, j, hp, hn: (jnp.maximum(i * r8 - 1, 0), j)),
                  pl.BlockSpec((tb, cb), lambda i, j, hp, hn: (i, j)),
                  pl.BlockSpec((SUBLANES, cb), lambda i, j, hp, hn: (jnp.minimum((i + 1) * r8, nb8 - 1), j)),
                  pl.BlockSpec((DN_CONV, cb), lambda i, j, hp, hn: (0, j))],
        out_specs=pl.BlockSpec((tb, cb), lambda i, j, hp, hn: (i, j)),
        scratch_shapes=[pltpu.VMEM((tb + 2 * SUBLANES, cb), F32)])
    return pl.pallas_call(
        functools.partial(_conv_kernel, kind_blocks=qk_width // cb, q_scale=dk ** -0.5),
        grid_spec=grid_spec,
        out_shape=jax.ShapeDtypeStruct((t, c), F32),
        compiler_params=_cparams("parallel", "parallel"), name="dn_conv",
    )(seq_lens["has_prev"], seq_lens["has_next"], qkvz, qkvz, qkvz, conv_w)


def _bdot(a, b):
    return _dot(a.astype(BF16), b.astype(BF16))


def _bdot_nt(a, b):
    return _dot_nt(a.astype(BF16), b.astype(BF16))


def _unit_tri_inverses(mats):
    n = mats[0].shape[0]
    row = lax.broadcasted_iota(jnp.int32, (n, 2 * n), 0)
    lane = lax.broadcasted_iota(jnp.int32, (n, 2 * n), 1)
    left = lane < n
    eye = (jnp.where(left, lane, lane - n) == row).astype(F32)

    def split(z):
        hi = z.astype(BF16)
        hi32 = hi.astype(F32)
        return hi, (z - hi32).astype(BF16), jnp.where(left, hi32, z - hi32).astype(BF16)

    def fold(r):
        r = r[:n] + r[n:]
        return r + pltpu.roll(r, n, 1)

    xs = [-a for a in mats]
    ts = [eye + x for x in xs]
    p = 1
    while True:
        square, update = 2 * p < n, p > 1
        new_xs, new_ts = [], []
        for x, t in zip(xs, ts):
            x_hi, x_lo, rhs = split(x)
            parts = [x_hi[:, :n], x_lo[:, :n]] if square else []
            if update:
                t_hi, t_lo, _ = split(t)
                parts += [t_hi[:, :n], t_lo[:, :n]]
            r = _dot(jnp.concatenate(parts, axis=0), rhs)
            new_xs.append(fold(r[:2 * n]) if square else None)
            new_ts.append(t + fold(r[-2 * n:]) if update else t)
        xs, ts = new_xs, new_ts
        if not square:
            return [t[:, :n] for t in ts]
        p *= 2


def _delta_kernel(blk_ref, seq_ref, first_ref, last_ref, q_ref, k_ref, v_ref, pc_ref, pr_ref, s0_ref, o_ref, sf_ref,
                  s_scr, *, hb, cg, reverse):
    n = pl.program_id(1)

    @pl.when(first_ref[n] == 1)
    def _():
        s_scr[...] = s0_ref[...]

    c = DN_CHUNK
    row = lax.broadcasted_iota(jnp.int32, (c, 2 * c), 0)
    col = lax.broadcasted_iota(jnp.int32, (c, 2 * c), 1)
    col = jnp.where(col < c, col, col - c)
    incl = (row <= col) if reverse else (row >= col)
    strict = (row < col) if reverse else (row > col)
    last = 0 if reverse else c - 1
    order = range(cg - 1, -1, -1) if reverse else range(cg)

    shared = {}
    for ci in order:
        rows = slice(ci * c, (ci + 1) * c)
        for kh in range(hb // 2):
            q = q_ref[rows, kh * LANES:(kh + 1) * LANES]
            k = k_ref[rows, kh * LANES:(kh + 1) * LANES]
            kb = k.astype(BF16)
            shared[ci, kh] = dict(q=q, k=k, kk=_dot_nt(kb, jnp.concatenate([kb, kb], axis=0)),
                                  qk=_dot_nt(q.astype(BF16), kb), kt=k.T)

    probs = [(ci, hh) for ci in order for hh in range(hb)]
    pre = {}
    for ci, hh in probs:
        pc = pc_ref[ci * c:(ci + 1) * c, :]
        pr = pr_ref[ci]
        beta = pc[:, hh:hh + 1]
        g_col = pc[:, hb + hh:hb + hh + 1]
        g_row2 = pr[hb + hh:hb + hh + 1, :]
        g_row = g_row2[:, :c]
        g_last = g_row[:, last:last + 1]
        decay2 = jnp.where(incl, jnp.exp(jnp.where(incl, g_col - g_row2, 0.0)), 0.0)
        pre[ci, hh] = dict(beta=beta, e_g=jnp.exp(g_col), decay=decay2[:, :c], gl=jnp.exp(g_last),
                           kscale=jnp.exp(g_last - g_row),
                           a=jnp.where(strict, beta * decay2 * shared[ci, hh // 2]["kk"], 0.0))
    tinvs = _unit_tri_inverses([pre[p]["a"] for p in probs])

    prep = {}
    for (ci, hh), tinv in zip(probs, tinvs):
        sh, pp = shared[ci, hh // 2], pre[ci, hh]
        v = v_ref[ci * c:(ci + 1) * c, hh * LANES:(hh + 1) * LANES]
        sol = _bdot(tinv, jnp.concatenate([pp["beta"] * v, (pp["beta"] * pp["e_g"]) * sh["k"]], axis=1))
        prep[ci, hh] = dict(
            u=sol[:, :LANES],
            wk_qg=jnp.concatenate([sol[:, LANES:], sh["q"] * pp["e_g"]], axis=0).astype(BF16),
            aqk=(pp["decay"] * sh["qk"]).astype(BF16),
            kdt=(sh["kt"] * pp["kscale"]).astype(BF16),
            gl=pp["gl"])

    states = [s_scr[hh] for hh in range(hb)]
    for ci in order:
        boths = [_dot(prep[ci, hh]["wk_qg"], states[hh].astype(BF16)) for hh in range(hb)]
        ws = [(prep[ci, hh]["u"] - boths[hh][:c]).astype(BF16) for hh in range(hb)]
        for hh in range(hb):
            o_ref[ci * c:(ci + 1) * c, hh * LANES:(hh + 1) * LANES] = (
                boths[hh][c:] + _dot(prep[ci, hh]["aqk"], ws[hh]))
        states = [prep[ci, hh]["gl"] * states[hh] + _dot(prep[ci, hh]["kdt"], ws[hh]) for hh in range(hb)]
    for hh in range(hb):
        s_scr[hh] = states[hh]

    @pl.when(last_ref[n] == 1)
    def _():
        sf_ref[...] = s_scr[...]


def _delta_scan(qkvc, p_col, p_row, s0, seq_lens, *, hk, hv, dk, dv, direction, hb, cg):
    assert dk == LANES and dv == LANES and hv == 2 * hk
    c = DN_CHUNK * cg
    hg = hv // hb
    reverse = direction == 1
    blk, seq, first, last = [], [], [], []
    start = 0
    for s, length in enumerate(seq_lens):
        nb = length // c
        ids = list(range(start, start + nb))
        blk += ids[::-1] if reverse else ids
        seq += [s] * nb
        first += [1] + [0] * (nb - 1)
        last += [0] * (nb - 1) + [1]
        start += nb
    tables = [jnp.asarray(np.asarray(a, np.int32)) for a in (blk, seq, first, last)]
    t = start * c

    qw = (hb // 2) * dk
    vw = hb * dv
    state_spec = pl.BlockSpec((None, hb, dk, dv), lambda g, n, bt, st, ft, lt: (st[n], g, 0, 0))
    grid_spec = pltpu.PrefetchScalarGridSpec(
        num_scalar_prefetch=4, grid=(hg, len(blk)),
        in_specs=[pl.BlockSpec((c, qw), lambda g, n, bt, st, ft, lt: (bt[n], g)),
                  pl.BlockSpec((c, qw), lambda g, n, bt, st, ft, lt: (bt[n], hk * dk // qw + g)),
                  pl.BlockSpec((c, vw), lambda g, n, bt, st, ft, lt: (bt[n], 2 * hk * dk // vw + g)),
                  pl.BlockSpec((None, None, c, 2 * hb), lambda g, n, bt, st, ft, lt: (g, direction, bt[n], 0)),
                  pl.BlockSpec((None, None, cg, 2 * hb, 2 * DN_CHUNK),
                               lambda g, n, bt, st, ft, lt: (g, direction, bt[n], 0, 0)),
                  state_spec],
        out_specs=[pl.BlockSpec((c, vw), lambda g, n, bt, st, ft, lt: (bt[n], g)), state_spec],
        scratch_shapes=[pltpu.VMEM((hb, dk, dv), F32)])
    return pl.pallas_call(
        functools.partial(_delta_kernel, hb=hb, cg=cg, reverse=reverse),
        grid_spec=grid_spec,
        out_shape=[jax.ShapeDtypeStruct((t, hv * dv), F32), jax.ShapeDtypeStruct(s0.shape, F32)],
        compiler_params=_cparams("parallel", "arbitrary"), name="dn_scan",
    )(*tables, qkvc, qkvc, qkvc, p_col, p_row, s0)


def _gated_norm_kernel(of_ref, ob_ref, z_ref, g_ref, o_ref):
    for c in range(o_ref.shape[1] // LANES):
        sl = slice(c * LANES, (c + 1) * LANES)
        o = of_ref[:, sl] + ob_ref[:, sl]
        o = o * lax.rsqrt(jnp.mean(o * o, axis=-1, keepdims=True) + EPS) * g_ref[...]
        o_ref[:, sl] = (o * _silu(z_ref[:, sl])).astype(o_ref.dtype)


def _gated_norm(o_f, o_b, qkvz, z_col0, norm_g):
    t, w = o_f.shape
    tb = _pick(t, 256)
    cb = _pick(math.gcd(w, z_col0), 1024)
    tok = pl.BlockSpec((tb, cb), lambda i, j: (i, j))
    return pl.pallas_call(
        _gated_norm_kernel, grid=(t // tb, w // cb),
        in_specs=[tok, tok, pl.BlockSpec((tb, cb), lambda i, j: (i, z_col0 // cb + j)),
                  pl.BlockSpec((1, LANES), lambda i, j: (0, 0))],
        out_specs=tok, out_shape=jax.ShapeDtypeStruct((t, w), BF16),
        compiler_params=_cparams("parallel", "parallel"), name="dn_gated_norm",
    )(o_f, o_b, qkvz, norm_g.reshape(1, LANES).astype(F32))


def _na_ctx_kernel(q_ref, k_ref, v_ref, o_ref, *, scale):
    q = (q_ref[...] * scale).astype(BF16)
    s = _dot_nt(q, k_ref[...].astype(BF16))
    p = jnp.exp(s - jnp.max(s, axis=-1, keepdims=True))
    o = _dot(p.astype(BF16), v_ref[...].astype(BF16)) / jnp.sum(p, axis=-1, keepdims=True)
    o_ref[...] = o.astype(o_ref.dtype)


def _na_context(qkv, n_seq, seq_len, heads):
    blk = lambda off: pl.BlockSpec((seq_len, LANES), lambda b, h: (b, off * heads + h))
    return pl.pallas_call(
        functools.partial(_na_ctx_kernel, scale=LANES ** -0.5),
        grid=(n_seq, heads), in_specs=[blk(0), blk(1), blk(2)],
        out_specs=pl.BlockSpec((seq_len, LANES), lambda b, h: (b, h)),
        out_shape=jax.ShapeDtypeStruct((n_seq * seq_len, heads * LANES), BF16),
        compiler_params=_cparams("parallel", "parallel"), name="na_ctx",
    )(qkv, qkv, qkv)


def _na_lat_kernel(q_ref, k_ref, v_ref, kc_ref, vc_ref, bias_ref, o_ref, kb_scr, vb_scr, *, rows, wr, scale):
    kb_scr[...] = k_ref[...].astype(BF16)
    vb_scr[...] = v_ref[...].astype(BF16)
    kc = kc_ref[...].astype(BF16)
    vc = vc_ref[...].astype(BF16)
    w = GRID_W

    group = math.gcd(rows, NA_ROWS_PER_STEP)

    def body(it, carry):
        rs = [it * group + k for k in range(group)]
        r0s = [jnp.clip(r - wr // 2, 0, rows - wr) for r in rs]
        qs = [(q_ref[pl.ds(pl.multiple_of(r * w, w), w), :] * scale).astype(BF16) for r in rs]
        wins = [pl.ds(pl.multiple_of(r0 * w, w), wr * w) for r0 in r0s]
        s_lats = [_dot_nt(q, kb_scr[win, :]) for q, win in zip(qs, wins)]
        s_ctxs = [_dot_nt(q, kc) for q in qs]
        outs = []
        for r, r0, win, s_lat, s_ctx in zip(rs, r0s, wins, s_lats, s_ctxs):
            d0 = r0 - r + WIN_R - 1
            s_lat = s_lat + jnp.concatenate([bias_ref[d0 + 2 * p] for p in range(wr // 2)], axis=1)
            m = jnp.maximum(jnp.max(s_lat, axis=-1, keepdims=True), jnp.max(s_ctx, axis=-1, keepdims=True))
            p_lat = jnp.exp(s_lat - m)
            p_ctx = jnp.exp(s_ctx - m)
            den = jnp.sum(p_lat, axis=-1, keepdims=True) + jnp.sum(p_ctx, axis=-1, keepdims=True)
            outs.append((p_lat.astype(BF16), p_ctx.astype(BF16), den))
        for r, win, (p_lat, p_ctx, den) in zip(rs, wins, outs):
            o = _dot(p_lat, vb_scr[win, :]) + _dot(p_ctx, vc)
            o_ref[pl.ds(pl.multiple_of(r * w, w), w), :] = (o / den).astype(o_ref.dtype)
        return carry

    lax.fori_loop(0, rows // group, body, 0)


def _na_bias_pairs(rpb):
    col = np.arange(GRID_W)
    c0 = np.clip(col - WIN_C // 2, 0, GRID_W - WIN_C)
    col_ok = (col[None, :] >= c0[:, None]) & (col[None, :] < c0[:, None] + WIN_C)
    dc_idx = np.clip(col[None, :] - col[:, None], 1 - WIN_C, WIN_C - 1) + WIN_C - 1
    bias = jnp.where(col_ok[None, None], rpb.astype(F32)[:, :, dc_idx], NEG)
    return jnp.concatenate([bias[:, :-1], bias[:, 1:]], axis=-1)


def _na_latent(qkv, cache_k, cache_v, rpb, row0, n_seq, seq_len, heads):
    rows = seq_len // GRID_W
    wr = min(WIN_R, rows)
    assert wr % 2 == 0 and row0 % seq_len == 0
    b0 = row0 // seq_len
    past = cache_k.shape[1]
    bias = _na_bias_pairs(rpb)
    blk = lambda off: pl.BlockSpec((seq_len, LANES), lambda b, h: (b0 + b, off * heads + h))
    cache = pl.BlockSpec((None, past, LANES), lambda b, h: (b, 0, h))
    return pl.pallas_call(
        functools.partial(_na_lat_kernel, rows=rows, wr=wr, scale=LANES ** -0.5),
        grid=(n_seq, heads),
        in_specs=[blk(0), blk(1), blk(2), cache, cache,
                  pl.BlockSpec((None,) + bias.shape[1:], lambda b, h: (h, 0, 0, 0))],
        out_specs=pl.BlockSpec((seq_len, LANES), lambda b, h: (b, h)),
        out_shape=jax.ShapeDtypeStruct((n_seq * seq_len, heads * LANES), BF16),
        scratch_shapes=[pltpu.VMEM((seq_len, LANES), BF16), pltpu.VMEM((seq_len, LANES), BF16)],
        compiler_params=_cparams("parallel", "parallel"), name="na_lat",
    )(qkv, qkv, qkv, cache_k, cache_v, bias)


def _top16(s, row):
    n = s.shape[0]
    vals = []
    rank = jnp.full(s.shape, PEER_TOPK, jnp.int32)
    for r in range(PEER_TOPK):
        m = jnp.max(s, axis=0, keepdims=True)
        idx = jnp.min(jnp.where(s == m, row, n), axis=0, keepdims=True)
        hit = row == idx
        vals.append(m)
        rank = jnp.where(hit, r, rank)
        s = jnp.where(hit, -jnp.inf, s)
    return jnp.concatenate(vals, axis=0), rank


def _peer_topk_kernel(q_ref, keys_ref, n_ref, e1_ref, b_ref, e2_ref, *, heads, half):
    tb = q_ref.shape[0]
    nk = keys_ref.shape[2]
    kk = PEER_TOPK
    assert kk == 2 * SUBLANES
    row = lax.broadcasted_iota(jnp.int32, (nk, tb), 0)
    row_k = lax.broadcasted_iota(jnp.int32, (kk, tb), 0)
    sub = lax.broadcasted_iota(jnp.int32, (SUBLANES, tb), 0)
    flat = jnp.concatenate([row_k] + [a * kk + sub for a in range(1, SUBLANES)]
                           + [(SUBLANES + g * SUBLANES + sub) * kk for g in range(kk // SUBLANES - 1)], axis=0)
    for h in range(heads):
        q1 = q_ref[:, (2 * h) * half:(2 * h + 1) * half]
        q2 = q_ref[:, (2 * h + 1) * half:(2 * h + 2) * half]
        s1 = _dot_nt(keys_ref[h, 0], q1, HI)
        s2 = _dot_nt(keys_ref[h, 1], q2, HI)
        v1, rank1 = _top16(s1, row)
        v2, rank2 = _top16(s2, row)
        cand = jnp.concatenate([v1[0:1, :] + v2] + [v1[a:a + 1, :] + v2[:SUBLANES] for a in range(1, SUBLANES)]
                               + [v1[SUBLANES:, :] + v2[0:1, :]], axis=0)
        cnt = jnp.zeros((kk, tb), jnp.int32)
        zsum = jnp.zeros((1, tb), F32)
        best0 = None
        for r in range(kk):
            m = jnp.max(cand, axis=0, keepdims=True)
            idx = jnp.min(jnp.where(cand == m, flat, kk * kk), axis=0, keepdims=True)
            best0 = m if r == 0 else best0
            zsum = zsum + jnp.exp(m - best0)
            cnt = cnt + (row_k == idx // kk).astype(jnp.int32)
            cand = jnp.where(flat == idx, -jnp.inf, cand)
        n_i = jnp.zeros((nk, tb), jnp.int32)
        for a in range(kk):
            n_i = jnp.where(rank1 == a, cnt[a:a + 1, :], n_i)
        n_ref[h] = n_i.astype(F32)
        e1_ref[h] = jnp.exp(s1 - v1[0:1, :]) / zsum
        b_ref[h] = pltpu.bitcast(rank2.astype(F32).astype(BF16), jnp.uint32)
        e2_ref[h] = pltpu.bitcast(jnp.exp(s2 - v2[0:1, :]).astype(BF16), jnp.uint32)


def _peer_topk(q, keys):
    t = q.shape[0]
    heads, _, nk, half = keys.shape
    tb = _pick(t, 256)
    out = [jax.ShapeDtypeStruct((heads, nk, t), F32)] * 2 + [jax.ShapeDtypeStruct((heads, nk // 2, t), jnp.uint32)] * 2
    spec = pl.BlockSpec((heads, nk, tb), lambda i: (0, 0, i))
    pspec = pl.BlockSpec((heads, nk // 2, tb), lambda i: (0, 0, i))
    return pl.pallas_call(
        functools.partial(_peer_topk_kernel, heads=heads, half=half),
        grid=(t // tb,),
        in_specs=[pl.BlockSpec((tb, q.shape[1]), lambda i: (i, 0)),
                  pl.BlockSpec(keys.shape, lambda i: (0, 0, 0, 0))],
        out_specs=[spec, spec, pspec, pspec], out_shape=out,
        compiler_params=_cparams("parallel"), name="peer_topk",
    )(q, keys)


def _gelu(x):
    return 0.5 * x * (1.0 + lax.erf(x * (2.0 ** -0.5)))


def _peer_dense_kernel(*refs, heads, ib, tw, parts):
    xt_ref = refs[0]
    u_refs = refs[1:1 + parts]
    vt_refs = refs[1 + parts:1 + 2 * parts]
    n_ref, e1_ref, b_ref, e2_ref, o_ref, acc_scr, a_scr, w_scr, cnt_scr, e1_scr = refs[1 + 2 * parts:]
    j = pl.program_id(1)

    @pl.when(j == 0)
    def _():
        acc_scr[...] = jnp.zeros(acc_scr.shape, F32)

    nk = 2 * b_ref.shape[1]
    d, tb = xt_ref.shape
    n_slab = tb // tw
    dp = d // parts

    def scores(s):
        a_scr[s] = sum(_dot(u_refs[p][...], xt_ref[p * dp:(p + 1) * dp, s * tw:(s + 1) * tw]) for p in range(parts))

    for h in range(heads):
        for ii in range(ib):
            cnt_scr[h, ii] = jnp.broadcast_to(n_ref[h, ii:ii + 1, :], (BF16_ROWS, tb)).astype(BF16)
            e1_scr[h, ii] = jnp.broadcast_to(e1_ref[h, ii:ii + 1, :], (BF16_ROWS, tb)).astype(BF16)

    def gate_tiles(s):
        rep = nk // BF16_ROWS
        for ii in range(ib):
            rows = slice(ii * nk, (ii + 1) * nk)
            for tc in range(tw // LANES):
                lanes = slice(s * tw + tc * LANES, s * tw + (tc + 1) * LANES)
                g = jnp.zeros((rep, BF16_ROWS, LANES), BF16)
                for h in range(heads):
                    cnt = cnt_scr[h, ii, :, lanes][None]
                    e1 = e1_scr[h, ii, :, lanes][None]
                    b = pltpu.bitcast(b_ref[h, :, lanes], BF16).reshape(g.shape)
                    e2 = pltpu.bitcast(e2_ref[h, :, lanes], BF16).reshape(g.shape)
                    g = g + jnp.where(b < cnt, e1 * e2, jnp.zeros((), BF16))
                a = a_scr[s, rows, tc * LANES:(tc + 1) * LANES]
                w_scr[s, rows, tc * LANES:(tc + 1) * LANES] = g.reshape(nk, LANES) * _gelu(a).astype(BF16)

    scores(0)
    for s in range(n_slab):
        if s + 1 < n_slab:
            scores(s + 1)
        gate_tiles(s)
        for p in range(parts):
            acc_scr[p * dp:(p + 1) * dp, s * tw:(s + 1) * tw] += _dot(vt_refs[p][...], w_scr[s])

    @pl.when(j == pl.num_programs(1) - 1)
    def _():
        o_ref[...] = acc_scr[...].T


def _peer_dense(ht, u_tab, v_tab, sel):
    d, t = ht.shape
    ne = u_tab.shape[0]
    n_i, e1, b_j, e2 = sel
    heads, nk, _ = n_i.shape
    tb = _pick(t, 512)
    tw = _pick(tb, MXU_WIDTH)
    ib = SUBLANES
    eb = ib * nk
    u_tab = u_tab.astype(BF16)
    vt_tab = v_tab.reshape(ne // eb, eb, d).transpose(0, 2, 1).astype(BF16)
    n_i = n_i.reshape(heads, nk // ib, ib, t)
    e1 = e1.reshape(heads, nk // ib, ib, t)
    row_spec = pl.BlockSpec((heads, None, ib, tb), lambda i, j: (0, j, 0, i))
    col_spec = pl.BlockSpec((heads, nk // 2, tb), lambda i, j: (0, 0, i))
    parts = TABLE_PARTS
    dp = d // parts
    u_specs = [pl.BlockSpec((eb, dp), functools.partial(lambda i, j, p: (j, p), p=p)) for p in range(parts)]
    vt_specs = [pl.BlockSpec((None, dp, eb), functools.partial(lambda i, j, p: (j, p, 0), p=p)) for p in range(parts)]
    return pl.pallas_call(
        functools.partial(_peer_dense_kernel, heads=heads, ib=ib, tw=tw, parts=parts),
        grid=(t // tb, ne // eb),
        in_specs=[pl.BlockSpec((d, tb), lambda i, j: (0, i))] + u_specs + vt_specs
        + [row_spec, row_spec, col_spec, col_spec],
        out_specs=pl.BlockSpec((tb, d), lambda i, j: (i, 0)),
        out_shape=jax.ShapeDtypeStruct((t, d), F32),
        scratch_shapes=[pltpu.VMEM((d, tb), F32), pltpu.VMEM((tb // tw, eb, tw), F32),
                        pltpu.VMEM((tb // tw, eb, tw), BF16),
                        pltpu.VMEM((heads, ib, BF16_ROWS, tb), BF16), pltpu.VMEM((heads, ib, BF16_ROWS, tb), BF16)],
        compiler_params=_cparams("parallel", "arbitrary"), name="peer_dense",
    )(ht, *([u_tab] * parts), *([vt_tab] * parts), n_i, e1, b_j, e2)


def _peer(h, ht, w_q, keys, u_tab, v_tab):
    q = _matmul(h, w_q.astype(BF16))
    sel = _peer_topk(q, keys.astype(F32))
    return _peer_dense(ht, u_tab, v_tab, sel)


def _seq_flags(groups, n_ctx_seq, tb):
    ctx_len = groups.ctx_tokens // n_ctx_seq
    starts = np.concatenate([np.arange(n_ctx_seq) * ctx_len,
                             groups.ctx_tokens + np.arange(groups.n_lat) * groups.lat_len])
    ends = np.concatenate([starts[1:], [groups.total]])
    blk = np.arange(groups.total // tb) * tb
    has_prev = ~np.isin(blk, starts)
    has_next = ~np.isin(blk + tb, ends)
    return {"tb": tb, "has_prev": jnp.asarray(has_prev, jnp.int32), "has_next": jnp.asarray(has_next, jnp.int32)}


def _deltanet(h, x, gate, groups, n_ctx_seq, w_in, conv_w, a_log, dt_bias, norm_g, w_o, state_delta):
    hv = a_log.shape[-1]
    dk, dv = state_delta.shape[-2:]
    qkv_w = conv_w.shape[1]
    v_w = hv * dv
    hk = (qkv_w - v_w) // (2 * dk)
    ctx_len = groups.ctx_tokens // n_ctx_seq
    t = groups.total
    c = DN_CHUNK

    qkvz = _matmul(h, w_in.astype(BF16), n_cols=qkv_w + v_w)
    p = _gates(h, w_in[:, qkv_w + v_w:].astype(F32), a_log, dt_bias)
    tb = _pick(math.gcd(ctx_len, groups.lat_len), 256)
    qkvc = _short_conv(qkvz, conv_w.astype(F32), _seq_flags(groups, n_ctx_seq, tb), hk * dk, dk)

    hb = min(DN_HEADS_PER_STEP, hv)
    cg = DN_CHUNKS_PER_STEP
    hg = hv // hb
    p5 = p.reshape(t, 2, 2, hg, hb).transpose(3, 2, 0, 1, 4).reshape(hg, 2, t, 2 * hb)
    p_row = p5.reshape(hg, 2, t // c, c, 2 * hb).transpose(0, 1, 2, 4, 3)
    p_row = jnp.concatenate([p_row, p_row], axis=-1)
    seq_lens = [ctx_len] * n_ctx_seq + [groups.lat_len] * groups.n_lat
    zero_state = jnp.zeros((n_ctx_seq, hv, dk, dv), F32)
    outs, states = [], []
    for direction in range(2):
        s0 = jnp.concatenate([zero_state, state_delta[:, direction].astype(F32)], axis=0)
        o, s_fin = _delta_scan(qkvc, p5, p_row, s0, seq_lens, hk=hk, hv=hv, dk=dk, dv=dv, direction=direction,
                               hb=hb, cg=cg)
        outs.append(o)
        states.append(s_fin[:n_ctx_seq])
    og = _gated_norm(outs[0], outs[1], qkvz, qkv_w, norm_g)
    x = _matmul(og, w_o.astype(BF16), groups=groups, res=x, gate=gate)
    return x, jnp.stack(states, axis=1)


def _natten(h, x, gate, groups, n_ctx_seq, w_qkv, rpb, w_o, cache_k, cache_v):
    d = h.shape[1]
    heads = d // LANES
    ctx_len = groups.ctx_tokens // n_ctx_seq
    qkv = _matmul(h, w_qkv.astype(BF16))
    o_c = _na_context(qkv, n_ctx_seq, ctx_len, heads)
    n_lat, past = cache_k.shape[:2]
    o_l = _na_latent(qkv, cache_k.reshape(n_lat, past, d), cache_v.reshape(n_lat, past, d), rpb,
                     groups.ctx_tokens, groups.n_lat, groups.lat_len, heads)
    x = _matmul(jnp.concatenate([o_c, o_l], axis=0), w_o.astype(BF16), groups=groups, res=x, gate=gate)
    k_c = qkv[:groups.ctx_tokens, d:2 * d].reshape(n_ctx_seq, ctx_len, heads, LANES)
    v_c = qkv[:groups.ctx_tokens, 2 * d:].reshape(n_ctx_seq, ctx_len, heads, LANES)
    return x, k_c, v_c


def kernel(x_prompt, x_sample, c, state_delta, cache_k, cache_v, c_ctx, ada_w, ada_b, norm1_g, norm2_g, final_g,
           dn_w_in, dn_conv_w, dn_a_log, dn_dt_bias, dn_norm_g, dn_w_o, na_w_qkv, na_rpb, na_w_o,
           peer_w_q, peer_keys, peer_u, peer_v):
    n_ctx_seq, ctx_len, d = x_prompt.shape
    n_lat, lat_len, _ = x_sample.shape
    depth = ada_w.shape[0]
    groups = _Groups(n_ctx_seq * ctx_len, n_lat, lat_len)
    n_grp = 1 + n_lat

    x = jnp.concatenate([x_prompt.reshape(-1, d), x_sample.reshape(-1, d)], axis=0).astype(F32)
    pad = (-n_grp) % SUBLANES
    cvecs = jnp.concatenate([c_ctx[None], c, jnp.zeros((pad, d), c.dtype)], axis=0).astype(F32)
    mods = _adaln(cvecs, ada_w.astype(F32), ada_b.astype(F32))[:, :n_grp]
    mods = mods.reshape(depth, n_grp, 6, 1, d).transpose(0, 2, 1, 3, 4)

    states, ctx_k, ctx_v = [], [], []
    delta = gate = None
    for i in range(depth):
        sh1, sc1, g1, sh2, sc2, g2 = (mods[i, k] for k in range(6))
        j = i // 2
        x, h = _normmod(x, norm1_g[i].astype(F32), groups, delta, gate, sh1, sc1)
        if i % 2 == 0:
            x, s_fin = _deltanet(h, x, g1, groups, n_ctx_seq, dn_w_in[j], dn_conv_w[j], dn_a_log[j], dn_dt_bias[j],
                                 dn_norm_g[j], dn_w_o[j], state_delta[:, j])
            states.append(s_fin)
        else:
            x, k_c, v_c = _natten(h, x, g1, groups, n_ctx_seq, na_w_qkv[j], na_rpb[j], na_w_o[j],
                                  cache_k[:, j], cache_v[:, j])
            ctx_k.append(k_c)
            ctx_v.append(v_c)
        x, h, ht = _normmod(x, norm2_g[i].astype(F32), groups, None, None, sh2, sc2, transposed=True)
        delta, gate = _peer(h, ht, peer_w_q[i], peer_keys[i], peer_u[i], peer_v[i]), g2
    final = functools.partial(_normmod, x, final_g.astype(F32), groups, delta, gate, out_dtype=F32)
    y_prompt = final(rows=(0, groups.ctx_tokens)).reshape(x_prompt.shape)
    y_sample = final(rows=(groups.ctx_tokens, n_lat * lat_len)).reshape(x_sample.shape)
    return (y_prompt, y_sample, jnp.stack(states, axis=1), jnp.stack(ctx_k, axis=1), jnp.stack(ctx_v, axis=1))
```

```python
import functools
import math

import jax
import jax.numpy as jnp
import numpy as np
from jax import lax
from jax.experimental import pallas as pl
from jax.experimental.pallas import tpu as pltpu

GRID_W = 64
WIN_R = 8
WIN_C = 16
DN_CHUNK = 64
DN_CONV = 5
PEER_TOPK = 16
EPS = 1e-6
DN_HEADS_PER_STEP = 8
DN_CHUNKS_PER_STEP = 2
NA_ROWS_PER_STEP = 4
CONV_ROWS = 64
TABLE_PARTS = 2

LANES = 128
SUBLANES = 8
BF16_ROWS = 16
MXU_WIDTH = 256
VMEM_LIMIT = 56 * 1024 * 1024

F32 = jnp.float32
BF16 = jnp.bfloat16
HI = lax.Precision.HIGHEST
NEG = -1e30


def _cparams(*sem):
    return pltpu.CompilerParams(dimension_semantics=sem, vmem_limit_bytes=VMEM_LIMIT)


def _dot(a, b, precision=None):
    return jnp.dot(a, b, preferred_element_type=F32, precision=precision)


def _dot_nt(a, b, precision=None):
    return lax.dot_general(a, b, (((1,), (1,)), ((), ())), preferred_element_type=F32, precision=precision)


def _silu(x):
    return x * (1.0 / (1.0 + jnp.exp(-x)))


def _pick(n, pref):
    b = min(n, pref)
    while n % b:
        b -= 1
    return b


def _adaln_kernel(c_ref, w_ref, b_ref, o_ref):
    o_ref[...] = _dot(_silu(c_ref[...]), w_ref[...], HI) + b_ref[...]


def _adaln(cvecs, ada_w, ada_b):
    depth, d, n6 = ada_w.shape
    r = cvecs.shape[0]
    tn = _pick(n6, 1024)
    return pl.pallas_call(
        _adaln_kernel,
        grid=(depth, n6 // tn),
        in_specs=[pl.BlockSpec((r, d), lambda i, j: (0, 0)),
                  pl.BlockSpec((None, d, tn), lambda i, j: (i, 0, j)),
                  pl.BlockSpec((None, 1, tn), lambda i, j: (i, 0, j))],
        out_specs=pl.BlockSpec((None, r, tn), lambda i, j: (i, 0, j)),
        out_shape=jax.ShapeDtypeStruct((depth, r, n6), F32),
        compiler_params=_cparams("parallel", "parallel"),
        name="adaln",
    )(cvecs, ada_w, ada_b.reshape(depth, 1, n6))


def _normmod_kernel(*refs, has_delta, has_mod, emit_x, emit_t):
    it = iter(refs)
    x_ref = next(it)
    if has_delta:
        d_ref, gate_ref = next(it), next(it)
    g_ref = next(it)
    if has_mod:
        sh_ref, sc_ref = next(it), next(it)
    if emit_x:
        xo_ref = next(it)
    h_ref = next(it)
    x = x_ref[...]
    if has_delta:
        x = x + gate_ref[...] * d_ref[...]
    if emit_x:
        xo_ref[...] = x
    y = x * lax.rsqrt(jnp.mean(x * x, axis=-1, keepdims=True) + EPS) * g_ref[...]
    if has_mod:
        y = y * (1.0 + sc_ref[...]) + sh_ref[...]
    h_ref[...] = y.astype(h_ref.dtype)
    if emit_t:
        next(it)[...] = y.T.astype(h_ref.dtype)


def _normmod(x, gain, groups, delta=None, gate=None, shift=None, scale=None, out_dtype=BF16, transposed=False,
             rows=None):
    t, d = x.shape
    tb = groups.block(256)
    row0, t_out = (0, t) if rows is None else rows
    b0 = row0 // tb
    has_delta, has_mod = delta is not None, shift is not None
    emit_x = has_delta and rows is None
    tok = pl.BlockSpec((tb, d), lambda i: (b0 + i, 0))
    out_tok = pl.BlockSpec((tb, d), lambda i: (i, 0))
    grp = pl.BlockSpec((None, 1, d), lambda i: (groups.of_block(b0 + i, tb), 0, 0))
    args, specs = [x], [tok]
    if has_delta:
        args += [delta, gate]
        specs += [tok, grp]
    args.append(gain.reshape(1, d))
    specs.append(pl.BlockSpec((1, d), lambda i: (0, 0)))
    if has_mod:
        args += [shift, scale]
        specs += [grp, grp]
    out_shape, out_specs = [], []
    if emit_x:
        out_shape.append(jax.ShapeDtypeStruct((t, d), F32))
        out_specs.append(out_tok)
    out_shape.append(jax.ShapeDtypeStruct((t_out, d), out_dtype))
    out_specs.append(out_tok)
    if transposed:
        out_shape.append(jax.ShapeDtypeStruct((d, t_out), out_dtype))
        out_specs.append(pl.BlockSpec((d, tb), lambda i: (0, i)))
    res = pl.pallas_call(
        functools.partial(_normmod_kernel, has_delta=has_delta, has_mod=has_mod, emit_x=emit_x,
                          emit_t=transposed),
        grid=(t_out // tb,), in_specs=specs, out_specs=out_specs, out_shape=out_shape,
        compiler_params=_cparams("parallel"), name="normmod",
    )(*args)
    if rows is not None:
        return res[0]
    res = list(res) if has_delta else [x] + list(res)
    return tuple(res)


class _Groups:
    def __init__(self, ctx_tokens, n_lat, lat_len):
        self.ctx_tokens, self.n_lat, self.lat_len = ctx_tokens, n_lat, lat_len
        self.total = ctx_tokens + n_lat * lat_len

    def block(self, pref):
        return _pick(math.gcd(self.ctx_tokens, self.lat_len), pref)

    def of_block(self, i, tb):
        nctx = self.ctx_tokens // tb
        return jnp.where(i < nctx, 0, 1 + (i - nctx) // (self.lat_len // tb))


def _matmul_kernel(*refs, has_res):
    if has_res:
        a_ref, w_ref, r_ref, g_ref, o_ref = refs
        o_ref[...] = r_ref[...] + g_ref[...] * _dot(a_ref[...], w_ref[...])
    else:
        a_ref, w_ref, o_ref = refs
        o_ref[...] = _dot(a_ref[...], w_ref[...]).astype(o_ref.dtype)


def _matmul(a, w, groups=None, res=None, gate=None, out_dtype=F32, n_cols=None, col0=0):
    m, k = a.shape
    n = w.shape[1] if n_cols is None else n_cols
    tm = _pick(m, 1024) if groups is None else groups.block(1024)
    tn = _pick(math.gcd(n, col0) if col0 else n, 512)
    cb0 = col0 // tn
    has_res = res is not None
    args = [a, w]
    specs = [pl.BlockSpec((tm, k), lambda i, j: (i, 0)), pl.BlockSpec((k, tn), lambda i, j: (0, cb0 + j))]
    if has_res:
        args += [res, gate]
        specs += [pl.BlockSpec((tm, tn), lambda i, j: (i, j)),
                  pl.BlockSpec((None, 1, tn), lambda i, j: (groups.of_block(i, tm), 0, j))]
    return pl.pallas_call(
        functools.partial(_matmul_kernel, has_res=has_res),
        grid=(m // tm, n // tn), in_specs=specs,
        out_specs=pl.BlockSpec((tm, tn), lambda i, j: (i, j)),
        out_shape=jax.ShapeDtypeStruct((m, n), out_dtype),
        compiler_params=_cparams("parallel", "parallel"), name="matmul",
    )(*args)


def _gates_kernel(h_ref, w_ref, alog_ref, dtb_ref, p_ref, *, hv, n_chunks):
    lane = lax.broadcasted_iota(jnp.int32, (DN_CHUNK, 4 * hv), 1)
    row = lax.broadcasted_iota(jnp.int32, (DN_CHUNK, DN_CHUNK), 0)
    col = lax.broadcasted_iota(jnp.int32, (DN_CHUNK, DN_CHUNK), 1)
    tri_lo = (row >= col).astype(F32)
    tri_up = (row <= col).astype(F32)
    bwd = ((lane // hv) % 2) == 1
    for c in range(n_chunks):
        sl = slice(c * DN_CHUNK, (c + 1) * DN_CHUNK)
        ba = _dot(h_ref[sl, :].astype(F32), w_ref[...], HI)
        beta = 1.0 / (1.0 + jnp.exp(-ba))
        z = ba + dtb_ref[...]
        softplus = jnp.maximum(z, 0.0) + jnp.log(1.0 + jnp.exp(-jnp.abs(z)))
        g = -jnp.exp(alog_ref[...]) * softplus
        cum = jnp.where(bwd, _dot(tri_up, g, HI), _dot(tri_lo, g, HI))
        p_ref[sl, :] = jnp.where(lane < 2 * hv, beta, cum)


def _gates(h, w_ba, a_log, dt_bias):
    t, d = h.shape
    hv = a_log.shape[-1]
    tb = _pick(t, 256)
    zeros = jnp.zeros((1, 2 * hv), F32)
    alog_row = jnp.concatenate([zeros, a_log.reshape(1, 2 * hv).astype(F32)], axis=1)
    dtb_row = jnp.concatenate([zeros, dt_bias.reshape(1, 2 * hv).astype(F32)], axis=1)
    return pl.pallas_call(
        functools.partial(_gates_kernel, hv=hv, n_chunks=tb // DN_CHUNK),
        grid=(t // tb,),
        in_specs=[pl.BlockSpec((tb, d), lambda i: (i, 0)),
                  pl.BlockSpec((d, 4 * hv), lambda i: (0, 0)),
                  pl.BlockSpec((1, 4 * hv), lambda i: (0, 0)),
                  pl.BlockSpec((1, 4 * hv), lambda i: (0, 0))],
        out_specs=pl.BlockSpec((tb, 4 * hv), lambda i: (i, 0)),
        out_shape=jax.ShapeDtypeStruct((t, 4 * hv), F32),
        compiler_params=_cparams("parallel"), name="dn_gates",
    )(h, w_ba, alog_row, dtb_row)


def _conv_kernel(hp_ref, hn_ref, prev_ref, cur_ref, next_ref, w_ref, o_ref, ext_scr, *, kind_blocks, q_scale):
    i, j = pl.program_id(0), pl.program_id(1)
    tb, cb = cur_ref.shape
    half = DN_CONV // 2
    ext_scr[:SUBLANES] = prev_ref[...] * hp_ref[i].astype(F32)
    ext_scr[SUBLANES:SUBLANES + tb] = cur_ref[...]
    ext_scr[SUBLANES + tb:] = next_ref[...] * hn_ref[i].astype(F32)
    rc = math.gcd(tb, CONV_ROWS)

    def tiles(finish):
        for c in range(cb // LANES):
            lanes = slice(c * LANES, (c + 1) * LANES)
            for r in range(tb // rc):
                acc = None
                for tap in range(DN_CONV):
                    start = SUBLANES + r * rc + tap - half
                    term = ext_scr[start:start + rc, lanes] * w_ref[tap:tap + 1, lanes]
                    acc = term if acc is None else acc + term
                o_ref[r * rc:(r + 1) * rc, lanes] = finish(_silu(acc))

    def l2(scale):
        return lambda y: y * (lax.rsqrt(jnp.sum(y * y, axis=-1, keepdims=True) + EPS) * scale)

    @pl.when(j < kind_blocks)
    def _():
        tiles(l2(q_scale))

    @pl.when(jnp.logical_and(j >= kind_blocks, j < 2 * kind_blocks))
    def _():
        tiles(l2(1.0))

    @pl.when(j >= 2 * kind_blocks)
    def _():
        tiles(lambda y: y)


def _short_conv(qkvz, conv_w, seq_lens, qk_width, dk):
    t = qkvz.shape[0]
    c = conv_w.shape[1]
    tb = seq_lens["tb"]
    cb = _pick(qk_width, 1024)
    nb8 = t // SUBLANES
    r8 = tb // SUBLANES
    grid_spec = pltpu.PrefetchScalarGridSpec(
        num_scalar_prefetch=2, grid=(t // tb, c // cb),
        in_specs=[pl.BlockSpec((SUBLANES, cb), lambda i, j, hp, hn: (jnp.maximum(i * r8 - 1, 0), j)),
                  pl.BlockSpec((tb, cb), lambda i, j, hp, hn: (i, j)),
                  pl.BlockSpec((SUBLANES, cb), lambda i, j, hp, hn: (jnp.minimum((i + 1) * r8, nb8 - 1), j)),
                  pl.BlockSpec((DN_CONV, cb), lambda i, j, hp, hn: (0, j))],
        out_specs=pl.BlockSpec((tb, cb), lambda i, j, hp, hn: (i, j)),
        scratch_shapes=[pltpu.VMEM((tb + 2 * SUBLANES, cb), F32)])
    return pl.pallas_call(
        functools.partial(_conv_kernel, kind_blocks=qk_width // cb, q_scale=dk ** -0.5),
        grid_spec=grid_spec,
        out_shape=jax.ShapeDtypeStruct((t, c), F32),
        compiler_params=_cparams("parallel", "parallel"), name="dn_conv",
    )(seq_lens["has_prev"], seq_lens["has_next"], qkvz, qkvz, qkvz, conv_w)


def _bdot(a, b):
    return _dot(a.astype(BF16), b.astype(BF16))


def _bdot_nt(a, b):
    return _dot_nt(a.astype(BF16), b.astype(BF16))


def _unit_tri_inverses(mats):
    n = mats[0].shape[0]
    row = lax.broadcasted_iota(jnp.int32, (n, 2 * n), 0)
    lane = lax.broadcasted_iota(jnp.int32, (n, 2 * n), 1)
    left = lane < n
    eye = (jnp.where(left, lane, lane - n) == row).astype(F32)

    def split(z):
        hi = z.astype(BF16)
        hi32 = hi.astype(F32)
        return hi, (z - hi32).astype(BF16), jnp.where(left, hi32, z - hi32).astype(BF16)

    def fold(r):
        r = r[:n] + r[n:]
        return r + pltpu.roll(r, n, 1)

    xs = [-a for a in mats]
    ts = [eye + x for x in xs]
    p = 1
    while True:
        square, update = 2 * p < n, p > 1
        new_xs, new_ts = [], []
        for x, t in zip(xs, ts):
            x_hi, x_lo, rhs = split(x)
            parts = [x_hi[:, :n], x_lo[:, :n]] if square else []
            if update:
                t_hi, t_lo, _ = split(t)
                parts += [t_hi[:, :n], t_lo[:, :n]]
            r = _dot(jnp.concatenate(parts, axis=0), rhs)
            new_xs.append(fold(r[:2 * n]) if square else None)
            new_ts.append(t + fold(r[-2 * n:]) if update else t)
        xs, ts = new_xs, new_ts
        if not square:
            return [t[:, :n] for t in ts]
        p *= 2


def _delta_kernel(blk_ref, seq_ref, first_ref, last_ref, q_ref, k_ref, v_ref, pc_ref, pr_ref, s0_ref, o_ref, sf_ref,
                  s_scr, *, hb, cg, reverse):
    n = pl.program_id(1)

    @pl.when(first_ref[n] == 1)
    def _():
        s_scr[...] = s0_ref[...]

    c = DN_CHUNK
    row = lax.broadcasted_iota(jnp.int32, (c, 2 * c), 0)
    col = lax.broadcasted_iota(jnp.int32, (c, 2 * c), 1)
    col = jnp.where(col < c, col, col - c)
    incl = (row <= col) if reverse else (row >= col)
    strict = (row < col) if reverse else (row > col)
    last = 0 if reverse else c - 1
    order = range(cg - 1, -1, -1) if reverse else range(cg)

    shared = {}
    for ci in order:
        rows = slice(ci * c, (ci + 1) * c)
        for kh in range(hb // 2):
            q = q_ref[rows, kh * LANES:(kh + 1) * LANES]
            k = k_ref[rows, kh * LANES:(kh + 1) * LANES]
            kb = k.astype(BF16)
            shared[ci, kh] = dict(q=q, k=k, kk=_dot_nt(kb, jnp.concatenate([kb, kb], axis=0)),
                                  qk=_dot_nt(q.astype(BF16), kb), kt=k.T)

    probs = [(ci, hh) for ci in order for hh in range(hb)]
    pre = {}
    for ci, hh in probs:
        pc = pc_ref[ci * c:(ci + 1) * c, :]
        pr = pr_ref[ci]
        beta = pc[:, hh:hh + 1]
        g_col = pc[:, hb + hh:hb + hh + 1]
        g_row2 = pr[hb + hh:hb + hh + 1, :]
        g_row = g_row2[:, :c]
        g_last = g_row[:, last:last + 1]
        decay2 = jnp.where(incl, jnp.exp(jnp.where(incl, g_col - g_row2, 0.0)), 0.0)
        pre[ci, hh] = dict(beta=beta, e_g=jnp.exp(g_col), decay=decay2[:, :c], gl=jnp.exp(g_last),
                           kscale=jnp.exp(g_last - g_row),
                           a=jnp.where(strict, beta * decay2 * shared[ci, hh // 2]["kk"], 0.0))
    tinvs = _unit_tri_inverses([pre[p]["a"] for p in probs])

    prep = {}
    for (ci, hh), tinv in zip(probs, tinvs):
        sh, pp = shared[ci, hh // 2], pre[ci, hh]
        v = v_ref[ci * c:(ci + 1) * c, hh * LANES:(hh + 1) * LANES]
        sol = _bdot(tinv, jnp.concatenate([pp["beta"] * v, (pp["beta"] * pp["e_g"]) * sh["k"]], axis=1))
        prep[ci, hh] = dict(
            u=sol[:, :LANES],
            wk_qg=jnp.concatenate([sol[:, LANES:], sh["q"] * pp["e_g"]], axis=0).astype(BF16),
            aqk=(pp["decay"] * sh["qk"]).astype(BF16),
            kdt=(sh["kt"] * pp["kscale"]).astype(BF16),
            gl=pp["gl"])

    states = [s_scr[hh] for hh in range(hb)]
    for ci in order:
        boths = [_dot(prep[ci, hh]["wk_qg"], states[hh].astype(BF16)) for hh in range(hb)]
        ws = [(prep[ci, hh]["u"] - boths[hh][:c]).astype(BF16) for hh in range(hb)]
        for hh in range(hb):
            o_ref[ci * c:(ci + 1) * c, hh * LANES:(hh + 1) * LANES] = (
                boths[hh][c:] + _dot(prep[ci, hh]["aqk"], ws[hh]))
        states = [prep[ci, hh]["gl"] * states[hh] + _dot(prep[ci, hh]["kdt"], ws[hh]) for hh in range(hb)]
    for hh in range(hb):
        s_scr[hh] = states[hh]

    @pl.when(last_ref[n] == 1)
    def _():
        sf_ref[...] = s_scr[...]


def _delta_scan(qkvc, p_col, p_row, s0, seq_lens, *, hk, hv, dk, dv, direction, hb, cg):
    assert dk == LANES and dv == LANES and hv == 2 * hk
    c = DN_CHUNK * cg
    hg = hv // hb
    reverse = direction == 1
    blk, seq, first, last = [], [], [], []
    start = 0
    for s, length in enumerate(seq_lens):
        nb = length // c
        ids = list(range(start, start + nb))
        blk += ids[::-1] if reverse else ids
        seq += [s] * nb
        first += [1] + [0] * (nb - 1)
        last += [0] * (nb - 1) + [1]
        start += nb
    tables = [jnp.asarray(np.asarray(a, np.int32)) for a in (blk, seq, first, last)]
    t = start * c

    qw = (hb // 2) * dk
    vw = hb * dv
    state_spec = pl.BlockSpec((None, hb, dk, dv), lambda g, n, bt, st, ft, lt: (st[n], g, 0, 0))
    grid_spec = pltpu.PrefetchScalarGridSpec(
        num_scalar_prefetch=4, grid=(hg, len(blk)),
        in_specs=[pl.BlockSpec((c, qw), lambda g, n, bt, st, ft, lt: (bt[n], g)),
                  pl.BlockSpec((c, qw), lambda g, n, bt, st, ft, lt: (bt[n], hk * dk // qw + g)),
                  pl.BlockSpec((c, vw), lambda g, n, bt, st, ft, lt: (bt[n], 2 * hk * dk // vw + g)),
                  pl.BlockSpec((None, None, c, 2 * hb), lambda g, n, bt, st, ft, lt: (g, direction, bt[n], 0)),
                  pl.BlockSpec((None, None, cg, 2 * hb, 2 * DN_CHUNK),
                               lambda g, n, bt, st, ft, lt: (g, direction, bt[n], 0, 0)),
                  state_spec],
        out_specs=[pl.BlockSpec((c, vw), lambda g, n, bt, st, ft, lt: (bt[n], g)), state_spec],
        scratch_shapes=[pltpu.VMEM((hb, dk, dv), F32)])
    return pl.pallas_call(
        functools.partial(_delta_kernel, hb=hb, cg=cg, reverse=reverse),
        grid_spec=grid_spec,
        out_shape=[jax.ShapeDtypeStruct((t, hv * dv), F32), jax.ShapeDtypeStruct(s0.shape, F32)],
        compiler_params=_cparams("parallel", "arbitrary"), name="dn_scan",
    )(*tables, qkvc, qkvc, qkvc, p_col, p_row, s0)


def _gated_norm_kernel(of_ref, ob_ref, z_ref, g_ref, o_ref):
    for c in range(o_ref.shape[1] // LANES):
        sl = slice(c * LANES, (c + 1) * LANES)
        o = of_ref[:, sl] + ob_ref[:, sl]
        o = o * lax.rsqrt(jnp.mean(o * o, axis=-1, keepdims=True) + EPS) * g_ref[...]
        o_ref[:, sl] = (o * _silu(z_ref[:, sl])).astype(o_ref.dtype)


def _gated_norm(o_f, o_b, qkvz, z_col0, norm_g):
    t, w = o_f.shape
    tb = _pick(t, 256)
    cb = _pick(math.gcd(w, z_col0), 1024)
    tok = pl.BlockSpec((tb, cb), lambda i, j: (i, j))
    return pl.pallas_call(
        _gated_norm_kernel, grid=(t // tb, w // cb),
        in_specs=[tok, tok, pl.BlockSpec((tb, cb), lambda i, j: (i, z_col0 // cb + j)),
                  pl.BlockSpec((1, LANES), lambda i, j: (0, 0))],
        out_specs=tok, out_shape=jax.ShapeDtypeStruct((t, w), BF16),
        compiler_params=_cparams("parallel", "parallel"), name="dn_gated_norm",
    )(o_f, o_b, qkvz, norm_g.reshape(1, LANES).astype(F32))


def _na_ctx_kernel(q_ref, k_ref, v_ref, o_ref, *, scale):
    q = (q_ref[...] * scale).astype(BF16)
    s = _dot_nt(q, k_ref[...].astype(BF16))
    p = jnp.exp(s - jnp.max(s, axis=-1, keepdims=True))
    o = _dot(p.astype(BF16), v_ref[...].astype(BF16)) / jnp.sum(p, axis=-1, keepdims=True)
    o_ref[...] = o.astype(o_ref.dtype)


def _na_context(qkv, n_seq, seq_len, heads):
    blk = lambda off: pl.BlockSpec((seq_len, LANES), lambda b, h: (b, off * heads + h))
    return pl.pallas_call(
        functools.partial(_na_ctx_kernel, scale=LANES ** -0.5),
        grid=(n_seq, heads), in_specs=[blk(0), blk(1), blk(2)],
        out_specs=pl.BlockSpec((seq_len, LANES), lambda b, h: (b, h)),
        out_shape=jax.ShapeDtypeStruct((n_seq * seq_len, heads * LANES), BF16),
        compiler_params=_cparams("parallel", "parallel"), name="na_ctx",
    )(qkv, qkv, qkv)


def _na_lat_kernel(q_ref, k_ref, v_ref, kc_ref, vc_ref, bias_ref, o_ref, kb_scr, vb_scr, *, rows, wr, scale):
    kb_scr[...] = k_ref[...].astype(BF16)
    vb_scr[...] = v_ref[...].astype(BF16)
    kc = kc_ref[...].astype(BF16)
    vc = vc_ref[...].astype(BF16)
    w = GRID_W

    group = math.gcd(rows, NA_ROWS_PER_STEP)

    def body(it, carry):
        rs = [it * group + k for k in range(group)]
        r0s = [jnp.clip(r - wr // 2, 0, rows - wr) for r in rs]
        qs = [(q_ref[pl.ds(pl.multiple_of(r * w, w), w), :] * scale).astype(BF16) for r in rs]
        wins = [pl.ds(pl.multiple_of(r0 * w, w), wr * w) for r0 in r0s]
        s_lats = [_dot_nt(q, kb_scr[win, :]) for q, win in zip(qs, wins)]
        s_ctxs = [_dot_nt(q, kc) for q in qs]
        outs = []
        for r, r0, win, s_lat, s_ctx in zip(rs, r0s, wins, s_lats, s_ctxs):
            d0 = r0 - r + WIN_R - 1
            s_lat = s_lat + jnp.concatenate([bias_ref[d0 + 2 * p] for p in range(wr // 2)], axis=1)
            m = jnp.maximum(jnp.max(s_lat, axis=-1, keepdims=True), jnp.max(s_ctx, axis=-1, keepdims=True))
            p_lat = jnp.exp(s_lat - m)
            p_ctx = jnp.exp(s_ctx - m)
            den = jnp.sum(p_lat, axis=-1, keepdims=True) + jnp.sum(p_ctx, axis=-1, keepdims=True)
            outs.append((p_lat.astype(BF16), p_ctx.astype(BF16), den))
        for r, win, (p_lat, p_ctx, den) in zip(rs, wins, outs):
            o = _dot(p_lat, vb_scr[win, :]) + _dot(p_ctx, vc)
            o_ref[pl.ds(pl.multiple_of(r * w, w), w), :] = (o / den).astype(o_ref.dtype)
        return carry

    lax.fori_loop(0, rows // group, body, 0)


def _na_bias_pairs(rpb):
    col = np.arange(GRID_W)
    c0 = np.clip(col - WIN_C // 2, 0, GRID_W - WIN_C)
    col_ok = (col[None, :] >= c0[:, None]) & (col[None, :] < c0[:, None] + WIN_C)
    dc_idx = np.clip(col[None, :] - col[:, None], 1 - WIN_C, WIN_C - 1) + WIN_C - 1
    bias = jnp.where(col_ok[None, None], rpb.astype(F32)[:, :, dc_idx], NEG)
    return jnp.concatenate([bias[:, :-1], bias[:, 1:]], axis=-1)


def _na_latent(qkv, cache_k, cache_v, rpb, row0, n_seq, seq_len, heads):
    rows = seq_len // GRID_W
    wr = min(WIN_R, rows)
    assert wr % 2 == 0 and row0 % seq_len == 0
    b0 = row0 // seq_len
    past = cache_k.shape[1]
    bias = _na_bias_pairs(rpb)
    blk = lambda off: pl.BlockSpec((seq_len, LANES), lambda b, h: (b0 + b, off * heads + h))
    cache = pl.BlockSpec((None, past, LANES), lambda b, h: (b, 0, h))
    return pl.pallas_call(
        functools.partial(_na_lat_kernel, rows=rows, wr=wr, scale=LANES ** -0.5),
        grid=(n_seq, heads),
        in_specs=[blk(0), blk(1), blk(2), cache, cache,
                  pl.BlockSpec((None,) + bias.shape[1:], lambda b, h: (h, 0, 0, 0))],
        out_specs=pl.BlockSpec((seq_len, LANES), lambda b, h: (b, h)),
        out_shape=jax.ShapeDtypeStruct((n_seq * seq_len, heads * LANES), BF16),
        scratch_shapes=[pltpu.VMEM((seq_len, LANES), BF16), pltpu.VMEM((seq_len, LANES), BF16)],
        compiler_params=_cparams("parallel", "parallel"), name="na_lat",
    )(qkv, qkv, qkv, cache_k, cache_v, bias)


def _top16(s, row):
    n = s.shape[0]
    vals = []
    rank = jnp.full(s.shape, PEER_TOPK, jnp.int32)
    for r in range(PEER_TOPK):
        m = jnp.max(s, axis=0, keepdims=True)
        idx = jnp.min(jnp.where(s == m, row, n), axis=0, keepdims=True)
        hit = row == idx
        vals.append(m)
        rank = jnp.where(hit, r, rank)
        s = jnp.where(hit, -jnp.inf, s)
    return jnp.concatenate(vals, axis=0), rank


def _top16_distinct(s):
    vals = []
    rank = jnp.full(s.shape, PEER_TOPK, jnp.int32)
    for r in range(PEER_TOPK):
        m = jnp.max(s, axis=0, keepdims=True)
        hit = s == m
        vals.append(m)
        rank = jnp.where(hit, r, rank)
        s = jnp.where(hit, -jnp.inf, s)
    count = jnp.sum((rank < PEER_TOPK).astype(jnp.int32), axis=0, keepdims=True)
    return jnp.concatenate(vals, axis=0), rank, count


def _top16_pair(s1, s2, row):
    v1, rank1, c1 = _top16_distinct(s1)
    v2, rank2, c2 = _top16_distinct(s2)
    tied = jnp.max(jnp.maximum(jnp.abs(c1 - PEER_TOPK), jnp.abs(c2 - PEER_TOPK)))
    return lax.cond(tied > 0, lambda: _top16(s1, row) + _top16(s2, row), lambda: (v1, rank1, v2, rank2))


def _peer_topk_kernel(q_ref, keys_ref, n_ref, e1_ref, b_ref, e2_ref, *, heads, half):
    tb = q_ref.shape[0]
    nk = keys_ref.shape[2]
    kk = PEER_TOPK
    assert kk == 2 * SUBLANES
    row = lax.broadcasted_iota(jnp.int32, (nk, tb), 0)
    row_k = lax.broadcasted_iota(jnp.int32, (kk, tb), 0)
    sub = lax.broadcasted_iota(jnp.int32, (SUBLANES, tb), 0)
    flat = jnp.concatenate([row_k] + [a * kk + sub for a in range(1, SUBLANES)]
                           + [(SUBLANES + g * SUBLANES + sub) * kk for g in range(kk // SUBLANES - 1)], axis=0)
    for h in range(heads):
        q1 = q_ref[:, (2 * h) * half:(2 * h + 1) * half]
        q2 = q_ref[:, (2 * h + 1) * half:(2 * h + 2) * half]
        s1 = _dot_nt(keys_ref[h, 0], q1, HI)
        s2 = _dot_nt(keys_ref[h, 1], q2, HI)
        v1, rank1, v2, rank2 = _top16_pair(s1, s2, row)
        cand = jnp.concatenate([v1[0:1, :] + v2] + [v1[a:a + 1, :] + v2[:SUBLANES] for a in range(1, SUBLANES)]
                               + [v1[SUBLANES:, :] + v2[0:1, :]], axis=0)
        cnt = jnp.zeros((kk, tb), jnp.int32)
        zsum = jnp.zeros((1, tb), F32)
        best0 = None
        for r in range(kk):
            m = jnp.max(cand, axis=0, keepdims=True)
            idx = jnp.min(jnp.where(cand == m, flat, kk * kk), axis=0, keepdims=True)
            best0 = m if r == 0 else best0
            zsum = zsum + jnp.exp(m - best0)
            cnt = cnt + (row_k == idx // kk).astype(jnp.int32)
            cand = jnp.where(flat == idx, -jnp.inf, cand)
        n_i = jnp.zeros((nk, tb), jnp.int32)
        for a in range(kk):
            n_i = jnp.where(rank1 == a, cnt[a:a + 1, :], n_i)
        n_ref[h] = n_i.astype(F32)
        e1_ref[h] = jnp.exp(s1 - v1[0:1, :]) / zsum
        b_ref[h] = pltpu.bitcast(rank2.astype(F32).astype(BF16), jnp.uint32)
        e2_ref[h] = pltpu.bitcast(jnp.exp(s2 - v2[0:1, :]).astype(BF16), jnp.uint32)


def _peer_topk(q, keys):
    t = q.shape[0]
    heads, _, nk, half = keys.shape
    tb = _pick(t, 256)
    out = [jax.ShapeDtypeStruct((heads, nk, t), F32)] * 2 + [jax.ShapeDtypeStruct((heads, nk // 2, t), jnp.uint32)] * 2
    spec = pl.BlockSpec((heads, nk, tb), lambda i: (0, 0, i))
    pspec = pl.BlockSpec((heads, nk // 2, tb), lambda i: (0, 0, i))
    return pl.pallas_call(
        functools.partial(_peer_topk_kernel, heads=heads, half=half),
        grid=(t // tb,),
        in_specs=[pl.BlockSpec((tb, q.shape[1]), lambda i: (i, 0)),
                  pl.BlockSpec(keys.shape, lambda i: (0, 0, 0, 0))],
        out_specs=[spec, spec, pspec, pspec], out_shape=out,
        compiler_params=_cparams("parallel"), name="peer_topk",
    )(q, keys)


def _gelu(x):
    return 0.5 * x * (1.0 + lax.erf(x * (2.0 ** -0.5)))


def _peer_dense_kernel(*refs, heads, ib, parts):
    xt_ref = refs[0]
    u_refs = refs[1:1 + parts]
    vt_refs = refs[1 + parts:1 + 2 * parts]
    n_ref, e1_ref, b_ref, e2_ref, o_ref, acc_scr, a_scr, w_scr, cnt_scr, e1_scr = refs[1 + 2 * parts:]
    j = pl.program_id(1)

    @pl.when(j == 0)
    def _():
        acc_scr[...] = jnp.zeros(acc_scr.shape, F32)

    nk = 2 * b_ref.shape[1]
    d, tb = xt_ref.shape
    n_slab = a_scr.shape[0]
    es = a_scr.shape[1]
    ipb = es // nk
    dp = d // parts

    def scores(s):
        a_scr[s] = sum(_dot(u_refs[p][s * es:(s + 1) * es, :], xt_ref[p * dp:(p + 1) * dp, :]) for p in range(parts))

    for h in range(heads):
        for ii in range(ib):
            cnt_scr[h, ii] = jnp.broadcast_to(n_ref[h, ii:ii + 1, :], (BF16_ROWS, tb)).astype(BF16)
            e1_scr[h, ii] = jnp.broadcast_to(e1_ref[h, ii:ii + 1, :], (BF16_ROWS, tb)).astype(BF16)

    def gate_tiles(s):
        rep = nk // BF16_ROWS
        for il in range(ipb):
            ii = s * ipb + il
            rows = slice(il * nk, (il + 1) * nk)
            for tc in range(tb // LANES):
                lanes = slice(tc * LANES, (tc + 1) * LANES)
                g = jnp.zeros((rep, BF16_ROWS, LANES), BF16)
                for h in range(heads):
                    cnt = cnt_scr[h, ii, :, lanes][None]
                    e1 = e1_scr[h, ii, :, lanes][None]
                    b = pltpu.bitcast(b_ref[h, :, lanes], BF16).reshape(g.shape)
                    e2 = pltpu.bitcast(e2_ref[h, :, lanes], BF16).reshape(g.shape)
                    g = g + jnp.where(b < cnt, e1 * e2, jnp.zeros((), BF16))
                w_scr[s, rows, lanes] = g.reshape(nk, LANES) * _gelu(a_scr[s, rows, lanes]).astype(BF16)

    scores(0)
    for s in range(n_slab):
        if s + 1 < n_slab:
            scores(s + 1)
        gate_tiles(s)
        for p in range(parts):
            acc_scr[p * dp:(p + 1) * dp, :] += _dot(vt_refs[p][:, s * es:(s + 1) * es], w_scr[s])

    @pl.when(j == pl.num_programs(1) - 1)
    def _():
        o_ref[...] = acc_scr[...].T


def _peer_dense(ht, u_tab, v_tab, sel):
    d, t = ht.shape
    ne = u_tab.shape[0]
    n_i, e1, b_j, e2 = sel
    heads, nk, _ = n_i.shape
    tb = _pick(t, 512)
    ib = SUBLANES
    eb = ib * nk
    n_slab = 2
    u_tab = u_tab.astype(BF16)
    vt_tab = v_tab.reshape(ne // eb, eb, d).transpose(0, 2, 1).astype(BF16)
    n_i = n_i.reshape(heads, nk // ib, ib, t)
    e1 = e1.reshape(heads, nk // ib, ib, t)
    row_spec = pl.BlockSpec((heads, None, ib, tb), lambda i, j: (0, j, 0, i))
    col_spec = pl.BlockSpec((heads, nk // 2, tb), lambda i, j: (0, 0, i))
    parts = TABLE_PARTS
    dp = d // parts
    u_specs = [pl.BlockSpec((eb, dp), functools.partial(lambda i, j, p: (j, p), p=p)) for p in range(parts)]
    vt_specs = [pl.BlockSpec((None, dp, eb), functools.partial(lambda i, j, p: (j, p, 0), p=p)) for p in range(parts)]
    return pl.pallas_call(
        functools.partial(_peer_dense_kernel, heads=heads, ib=ib, parts=parts),
        grid=(t // tb, ne // eb),
        in_specs=[pl.BlockSpec((d, tb), lambda i, j: (0, i))] + u_specs + vt_specs
        + [row_spec, row_spec, col_spec, col_spec],
        out_specs=pl.BlockSpec((tb, d), lambda i, j: (i, 0)),
        out_shape=jax.ShapeDtypeStruct((t, d), F32),
        scratch_shapes=[pltpu.VMEM((d, tb), F32), pltpu.VMEM((n_slab, eb // n_slab, tb), F32),
                        pltpu.VMEM((n_slab, eb // n_slab, tb), BF16),
                        pltpu.VMEM((heads, ib, BF16_ROWS, tb), BF16), pltpu.VMEM((heads, ib, BF16_ROWS, tb), BF16)],
        compiler_params=_cparams("parallel", "arbitrary"), name="peer_dense",
    )(ht, *([u_tab] * parts), *([vt_tab] * parts), n_i, e1, b_j, e2)


def _peer(h, ht, w_q, keys, u_tab, v_tab):
    q = _matmul(h, w_q.astype(BF16))
    sel = _peer_topk(q, keys.astype(F32))
    return _peer_dense(ht, u_tab, v_tab, sel)


def _seq_flags(groups, n_ctx_seq, tb):
    ctx_len = groups.ctx_tokens // n_ctx_seq
    starts = np.concatenate([np.arange(n_ctx_seq) * ctx_len,
                             groups.ctx_tokens + np.arange(groups.n_lat) * groups.lat_len])
    ends = np.concatenate([starts[1:], [groups.total]])
    blk = np.arange(groups.total // tb) * tb
    has_prev = ~np.isin(blk, starts)
    has_next = ~np.isin(blk + tb, ends)
    return {"tb": tb, "has_prev": jnp.asarray(has_prev, jnp.int32), "has_next": jnp.asarray(has_next, jnp.int32)}


def _deltanet(h, x, gate, groups, n_ctx_seq, w_in, conv_w, a_log, dt_bias, norm_g, w_o, state_delta):
    hv = a_log.shape[-1]
    dk, dv = state_delta.shape[-2:]
    qkv_w = conv_w.shape[1]
    v_w = hv * dv
    hk = (qkv_w - v_w) // (2 * dk)
    ctx_len = groups.ctx_tokens // n_ctx_seq
    t = groups.total
    c = DN_CHUNK

    qkvz = _matmul(h, w_in.astype(BF16), n_cols=qkv_w + v_w)
    p = _gates(h, w_in[:, qkv_w + v_w:].astype(F32), a_log, dt_bias)
    tb = _pick(math.gcd(ctx_len, groups.lat_len), 256)
    qkvc = _short_conv(qkvz, conv_w.astype(F32), _seq_flags(groups, n_ctx_seq, tb), hk * dk, dk)

    hb = min(DN_HEADS_PER_STEP, hv)
    cg = DN_CHUNKS_PER_STEP
    hg = hv // hb
    p5 = p.reshape(t, 2, 2, hg, hb).transpose(3, 2, 0, 1, 4).reshape(hg, 2, t, 2 * hb)
    p_row = p5.reshape(hg, 2, t // c, c, 2 * hb).transpose(0, 1, 2, 4, 3)
    p_row = jnp.concatenate([p_row, p_row], axis=-1)
    seq_lens = [ctx_len] * n_ctx_seq + [groups.lat_len] * groups.n_lat
    zero_state = jnp.zeros((n_ctx_seq, hv, dk, dv), F32)
    outs, states = [], []
    for direction in range(2):
        s0 = jnp.concatenate([zero_state, state_delta[:, direction].astype(F32)], axis=0)
        o, s_fin = _delta_scan(qkvc, p5, p_row, s0, seq_lens, hk=hk, hv=hv, dk=dk, dv=dv, direction=direction,
                               hb=hb, cg=cg)
        outs.append(o)
        states.append(s_fin[:n_ctx_seq])
    og = _gated_norm(outs[0], outs[1], qkvz, qkv_w, norm_g)
    x = _matmul(og, w_o.astype(BF16), groups=groups, res=x, gate=gate)
    return x, jnp.stack(states, axis=1)


def _natten(h, x, gate, groups, n_ctx_seq, w_qkv, rpb, w_o, cache_k, cache_v):
    d = h.shape[1]
    heads = d // LANES
    ctx_len = groups.ctx_tokens // n_ctx_seq
    qkv = _matmul(h, w_qkv.astype(BF16))
    o_c = _na_context(qkv, n_ctx_seq, ctx_len, heads)
    n_lat, past = cache_k.shape[:2]
    o_l = _na_latent(qkv, cache_k.reshape(n_lat, past, d), cache_v.reshape(n_lat, past, d), rpb,
                     groups.ctx_tokens, groups.n_lat, groups.lat_len, heads)
    x = _matmul(jnp.concatenate([o_c, o_l], axis=0), w_o.astype(BF16), groups=groups, res=x, gate=gate)
    k_c = qkv[:groups.ctx_tokens, d:2 * d].reshape(n_ctx_seq, ctx_len, heads, LANES)
    v_c = qkv[:groups.ctx_tokens, 2 * d:].reshape(n_ctx_seq, ctx_len, heads, LANES)
    return x, k_c, v_c


def kernel(x_prompt, x_sample, c, state_delta, cache_k, cache_v, c_ctx, ada_w, ada_b, norm1_g, norm2_g, final_g,
           dn_w_in, dn_conv_w, dn_a_log, dn_dt_bias, dn_norm_g, dn_w_o, na_w_qkv, na_rpb, na_w_o,
           peer_w_q, peer_keys, peer_u, peer_v):
    n_ctx_seq, ctx_len, d = x_prompt.shape
    n_lat, lat_len, _ = x_sample.shape
    depth = ada_w.shape[0]
    groups = _Groups(n_ctx_seq * ctx_len, n_lat, lat_len)
    n_grp = 1 + n_lat

    x = jnp.concatenate([x_prompt.reshape(-1, d), x_sample.reshape(-1, d)], axis=0).astype(F32)
    pad = (-n_grp) % SUBLANES
    cvecs = jnp.concatenate([c_ctx[None], c, jnp.zeros((pad, d), c.dtype)], axis=0).astype(F32)
    mods = _adaln(cvecs, ada_w.astype(F32), ada_b.astype(F32))[:, :n_grp]
    mods = mods.reshape(depth, n_grp, 6, 1, d).transpose(0, 2, 1, 3, 4)

    states, ctx_k, ctx_v = [], [], []
    delta = gate = None
    for i in range(depth):
        sh1, sc1, g1, sh2, sc2, g2 = (mods[i, k] for k in range(6))
        j = i // 2
        x, h = _normmod(x, norm1_g[i].astype(F32), groups, delta, gate, sh1, sc1)
        if i % 2 == 0:
            x, s_fin = _deltanet(h, x, g1, groups, n_ctx_seq, dn_w_in[j], dn_conv_w[j], dn_a_log[j], dn_dt_bias[j],
                                 dn_norm_g[j], dn_w_o[j], state_delta[:, j])
            states.append(s_fin)
        else:
            x, k_c, v_c = _natten(h, x, g1, groups, n_ctx_seq, na_w_qkv[j], na_rpb[j], na_w_o[j],
                                  cache_k[:, j], cache_v[:, j])
            ctx_k.append(k_c)
            ctx_v.append(v_c)
        x, h, ht = _normmod(x, norm2_g[i].astype(F32), groups, None, None, sh2, sc2, transposed=True)
        delta, gate = _peer(h, ht, peer_w_q[i], peer_keys[i], peer_u[i], peer_v[i]), g2
    final = functools.partial(_normmod, x, final_g.astype(F32), groups, delta, gate, out_dtype=F32)
    y_prompt = final(rows=(0, groups.ctx_tokens)).reshape(x_prompt.shape)
    y_sample = final(rows=(groups.ctx_tokens, n_lat * lat_len)).reshape(x_sample.shape)
    return (y_prompt, y_sample, jnp.stack(states, axis=1), jnp.stack(ctx_k, axis=1), jnp.stack(ctx_v, axis=1))
```

```python
import functools
import math

import jax
import jax.numpy as jnp
import numpy as np
from jax import lax
from jax.experimental import pallas as pl
from jax.experimental.pallas import tpu as pltpu

GRID_W = 64
WIN_R = 8
WIN_C = 16
DN_CHUNK = 64
DN_CONV = 5
PEER_TOPK = 16
EPS = 1e-6
DN_HEADS_PER_STEP = 8
DN_CHUNKS_PER_STEP = 2
NA_ROWS_PER_STEP = 4
CONV_ROWS = 64
TABLE_PARTS = 2

LANES = 128
SUBLANES = 8
BF16_ROWS = 16
MXU_WIDTH = 256
VMEM_LIMIT = 56 * 1024 * 1024

F32 = jnp.float32
BF16 = jnp.bfloat16
HI = lax.Precision.HIGHEST
NEG = -1e30


def _cparams(*sem):
    return pltpu.CompilerParams(dimension_semantics=sem, vmem_limit_bytes=VMEM_LIMIT)


def _dot(a, b, precision=None):
    return jnp.dot(a, b, preferred_element_type=F32, precision=precision)


def _dot_nt(a, b, precision=None):
    return lax.dot_general(a, b, (((1,), (1,)), ((), ())), preferred_element_type=F32, precision=precision)


def _silu(x):
    return x * (1.0 / (1.0 + jnp.exp(-x)))


def _pick(n, pref):
    b = min(n, pref)
    while n % b:
        b -= 1
    return b


def _adaln_kernel(c_ref, w_ref, b_ref, o_ref):
    o_ref[...] = _dot(_silu(c_ref[...]), w_ref[...], HI) + b_ref[...]


def _adaln(cvecs, ada_w, ada_b):
    depth, d, n6 = ada_w.shape
    r = cvecs.shape[0]
    tn = _pick(n6, 1024)
    return pl.pallas_call(
        _adaln_kernel,
        grid=(depth, n6 // tn),
        in_specs=[pl.BlockSpec((r, d), lambda i, j: (0, 0)),
                  pl.BlockSpec((None, d, tn), lambda i, j: (i, 0, j)),
                  pl.BlockSpec((None, 1, tn), lambda i, j: (i, 0, j))],
        out_specs=pl.BlockSpec((None, r, tn), lambda i, j: (i, 0, j)),
        out_shape=jax.ShapeDtypeStruct((depth, r, n6), F32),
        compiler_params=_cparams("parallel", "parallel"),
        name="adaln",
    )(cvecs, ada_w, ada_b.reshape(depth, 1, n6))


def _normmod_kernel(*refs, has_delta, has_mod, emit_x, emit_t):
    it = iter(refs)
    x_ref = next(it)
    if has_delta:
        d_ref, gate_ref = next(it), next(it)
    g_ref = next(it)
    if has_mod:
        sh_ref, sc_ref = next(it), next(it)
    if emit_x:
        xo_ref = next(it)
    h_ref = next(it)
    x = x_ref[...]
    if has_delta:
        x = x + gate_ref[...] * d_ref[...]
    if emit_x:
        xo_ref[...] = x
    y = x * lax.rsqrt(jnp.mean(x * x, axis=-1, keepdims=True) + EPS) * g_ref[...]
    if has_mod:
        y = y * (1.0 + sc_ref[...]) + sh_ref[...]
    h_ref[...] = y.astype(h_ref.dtype)
    if emit_t:
        next(it)[...] = y.T.astype(h_ref.dtype)


def _normmod(x, gain, groups, delta=None, gate=None, shift=None, scale=None, out_dtype=BF16, transposed=False,
             rows=None):
    t, d = x.shape
    tb = groups.block(256)
    row0, t_out = (0, t) if rows is None else rows
    b0 = row0 // tb
    has_delta, has_mod = delta is not None, shift is not None
    emit_x = has_delta and rows is None
    tok = pl.BlockSpec((tb, d), lambda i: (b0 + i, 0))
    out_tok = pl.BlockSpec((tb, d), lambda i: (i, 0))
    grp = pl.BlockSpec((None, 1, d), lambda i: (groups.of_block(b0 + i, tb), 0, 0))
    args, specs = [x], [tok]
    if has_delta:
        args += [delta, gate]
        specs += [tok, grp]
    args.append(gain.reshape(1, d))
    specs.append(pl.BlockSpec((1, d), lambda i: (0, 0)))
    if has_mod:
        args += [shift, scale]
        specs += [grp, grp]
    out_shape, out_specs = [], []
    if emit_x:
        out_shape.append(jax.ShapeDtypeStruct((t, d), F32))
        out_specs.append(out_tok)
    out_shape.append(jax.ShapeDtypeStruct((t_out, d), out_dtype))
    out_specs.append(out_tok)
    if transposed:
        out_shape.append(jax.ShapeDtypeStruct((d, t_out), out_dtype))
        out_specs.append(pl.BlockSpec((d, tb), lambda i: (0, i)))
    res = pl.pallas_call(
        functools.partial(_normmod_kernel, has_delta=has_delta, has_mod=has_mod, emit_x=emit_x,
                          emit_t=transposed),
        grid=(t_out // tb,), in_specs=specs, out_specs=out_specs, out_shape=out_shape,
        compiler_params=_cparams("parallel"), name="normmod",
    )(*args)
    if rows is not None:
        return res[0]
    res = list(res) if has_delta else [x] + list(res)
    return tuple(res)


class _Groups:
    def __init__(self, ctx_tokens, n_lat, lat_len):
        self.ctx_tokens, self.n_lat, self.lat_len = ctx_tokens, n_lat, lat_len
        self.total = ctx_tokens + n_lat * lat_len

    def block(self, pref):
        return _pick(math.gcd(self.ctx_tokens, self.lat_len), pref)

    def of_block(self, i, tb):
        nctx = self.ctx_tokens // tb
        return jnp.where(i < nctx, 0, 1 + (i - nctx) // (self.lat_len // tb))


def _matmul_kernel(*refs, has_res):
    if has_res:
        a_ref, w_ref, r_ref, g_ref, o_ref = refs
        o_ref[...] = r_ref[...] + g_ref[...] * _dot(a_ref[...], w_ref[...])
    else:
        a_ref, w_ref, o_ref = refs
        o_ref[...] = _dot(a_ref[...], w_ref[...]).astype(o_ref.dtype)


def _matmul(a, w, groups=None, res=None, gate=None, out_dtype=F32, n_cols=None, col0=0):
    m, k = a.shape
    n = w.shape[1] if n_cols is None else n_cols
    tm = _pick(m, 1024) if groups is None else groups.block(1024)
    tn = _pick(math.gcd(n, col0) if col0 else n, 512)
    cb0 = col0 // tn
    has_res = res is not None
    args = [a, w]
    specs = [pl.BlockSpec((tm, k), lambda i, j: (i, 0)), pl.BlockSpec((k, tn), lambda i, j: (0, cb0 + j))]
    if has_res:
        args += [res, gate]
        specs += [pl.BlockSpec((tm, tn), lambda i, j: (i, j)),
                  pl.BlockSpec((None, 1, tn), lambda i, j: (groups.of_block(i, tm), 0, j))]
    return pl.pallas_call(
        functools.partial(_matmul_kernel, has_res=has_res),
        grid=(m // tm, n // tn), in_specs=specs,
        out_specs=pl.BlockSpec((tm, tn), lambda i, j: (i, j)),
        out_shape=jax.ShapeDtypeStruct((m, n), out_dtype),
        compiler_params=_cparams("parallel", "parallel"), name="matmul",
    )(*args)


def _gates_kernel(h_ref, w_ref, alog_ref, dtb_ref, p_ref, *, hv, n_chunks):
    lane = lax.broadcasted_iota(jnp.int32, (DN_CHUNK, 4 * hv), 1)
    row = lax.broadcasted_iota(jnp.int32, (DN_CHUNK, DN_CHUNK), 0)
    col = lax.broadcasted_iota(jnp.int32, (DN_CHUNK, DN_CHUNK), 1)
    tri_lo = (row >= col).astype(F32)
    tri_up = (row <= col).astype(F32)
    bwd = ((lane // hv) % 2) == 1
    for c in range(n_chunks):
        sl = slice(c * DN_CHUNK, (c + 1) * DN_CHUNK)
        ba = _dot(h_ref[sl, :].astype(F32), w_ref[...], HI)
        beta = 1.0 / (1.0 + jnp.exp(-ba))
        z = ba + dtb_ref[...]
        softplus = jnp.maximum(z, 0.0) + jnp.log(1.0 + jnp.exp(-jnp.abs(z)))
        g = -jnp.exp(alog_ref[...]) * softplus
        cum = jnp.where(bwd, _dot(tri_up, g, HI), _dot(tri_lo, g, HI))
        p_ref[sl, :] = jnp.where(lane < 2 * hv, beta, cum)


def _gates(h, w_ba, a_log, dt_bias):
    t, d = h.shape
    hv = a_log.shape[-1]
    tb = _pick(t, 256)
    zeros = jnp.zeros((1, 2 * hv), F32)
    alog_row = jnp.concatenate([zeros, a_log.reshape(1, 2 * hv).astype(F32)], axis=1)
    dtb_row = jnp.concatenate([zeros, dt_bias.reshape(1, 2 * hv).astype(F32)], axis=1)
    return pl.pallas_call(
        functools.partial(_gates_kernel, hv=hv, n_chunks=tb // DN_CHUNK),
        grid=(t // tb,),
        in_specs=[pl.BlockSpec((tb, d), lambda i: (i, 0)),
                  pl.BlockSpec((d, 4 * hv), lambda i: (0, 0)),
                  pl.BlockSpec((1, 4 * hv), lambda i: (0, 0)),
                  pl.BlockSpec((1, 4 * hv), lambda i: (0, 0))],
        out_specs=pl.BlockSpec((tb, 4 * hv), lambda i: (i, 0)),
        out_shape=jax.ShapeDtypeStruct((t, 4 * hv), F32),
        compiler_params=_cparams("parallel"), name="dn_gates",
    )(h, w_ba, alog_row, dtb_row)


def _conv_kernel(hp_ref, hn_ref, prev_ref, cur_ref, next_ref, w_ref, o_ref, ext_scr, *, kind_blocks, q_scale):
    i, j = pl.program_id(0), pl.program_id(1)
    tb, cb = cur_ref.shape
    half = DN_CONV // 2
    ext_scr[:SUBLANES] = prev_ref[...] * hp_ref[i].astype(F32)
    ext_scr[SUBLANES:SUBLANES + tb] = cur_ref[...]
    ext_scr[SUBLANES + tb:] = next_ref[...] * hn_ref[i].astype(F32)
    rc = math.gcd(tb, CONV_ROWS)

    def tiles(finish):
        for c in range(cb // LANES):
            lanes = slice(c * LANES, (c + 1) * LANES)
            for r in range(tb // rc):
                acc = None
                for tap in range(DN_CONV):
                    start = SUBLANES + r * rc + tap - half
                    term = ext_scr[start:start + rc, lanes] * w_ref[tap:tap + 1, lanes]
                    acc = term if acc is None else acc + term
                o_ref[r * rc:(r + 1) * rc, lanes] = finish(_silu(acc))

    def l2(scale):
        return lambda y: y * (lax.rsqrt(jnp.sum(y * y, axis=-1, keepdims=True) + EPS) * scale)

    @pl.when(j < kind_blocks)
    def _():
        tiles(l2(q_scale))

    @pl.when(jnp.logical_and(j >= kind_blocks, j < 2 * kind_blocks))
    def _():
        tiles(l2(1.0))

    @pl.when(j >= 2 * kind_blocks)
    def _():
        tiles(lambda y: y)


def _short_conv(qkvz, conv_w, seq_lens, qk_width, dk):
    t = qkvz.shape[0]
    c = conv_w.shape[1]
    tb = seq_lens["tb"]
    cb = _pick(qk_width, 1024)
    nb8 = t // SUBLANES
    r8 = tb // SUBLANES
    grid_spec = pltpu.PrefetchScalarGridSpec(
        num_scalar_prefetch=2, grid=(t // tb, c // cb),
        in_specs=[pl.BlockSpec((SUBLANES, cb), lambda i, j, hp, hn: (jnp.maximum(i * r8 - 1, 0), j)),
                  pl.BlockSpec((tb, cb), lambda i, j, hp, hn: (i, j)),
                  pl.BlockSpec((SUBLANES, cb), lambda i, j, hp, hn: (jnp.minimum((i + 1) * r8, nb8 - 1), j)),
                  pl.BlockSpec((DN_CONV, cb), lambda i, j, hp, hn: (0, j))],
        out_specs=pl.BlockSpec((tb, cb), lambda i, j, hp, hn: (i, j)),
        scratch_shapes=[pltpu.VMEM((tb + 2 * SUBLANES, cb), F32)])
    return pl.pallas_call(
        functools.partial(_conv_kernel, kind_blocks=qk_width // cb, q_scale=dk ** -0.5),
        grid_spec=grid_spec,
        out_shape=jax.ShapeDtypeStruct((t, c), F32),
        compiler_params=_cparams("parallel", "parallel"), name="dn_conv",
    )(seq_lens["has_prev"], seq_lens["has_next"], qkvz, qkvz, qkvz, conv_w)


def _bdot(a, b):
    return _dot(a.astype(BF16), b.astype(BF16))


def _bdot_nt(a, b):
    return _dot_nt(a.astype(BF16), b.astype(BF16))


def _unit_tri_inverses(mats):
    n = mats[0].shape[0]
    row = lax.broadcasted_iota(jnp.int32, (n, 2 * n), 0)
    lane = lax.broadcasted_iota(jnp.int32, (n, 2 * n), 1)
    left = lane < n
    eye = (jnp.where(left, lane, lane - n) == row).astype(F32)

    def split(z):
        hi = z.astype(BF16)
        hi32 = hi.astype(F32)
        return hi, (z - hi32).astype(BF16), jnp.where(left, hi32, z - hi32).astype(BF16)

    def fold(r):
        r = r[:n] + r[n:]
        return r + pltpu.roll(r, n, 1)

    xs = [-a for a in mats]
    ts = [eye + x for x in xs]
    p = 1
    while True:
        square, update = 2 * p < n, p > 1
        new_xs, new_ts = [], []
        for x, t in zip(xs, ts):
            x_hi, x_lo, rhs = split(x)
            parts = [x_hi[:, :n], x_lo[:, :n]] if square else []
            if update:
                t_hi, t_lo, _ = split(t)
                parts += [t_hi[:, :n], t_lo[:, :n]]
            r = _dot(jnp.concatenate(parts, axis=0), rhs)
            new_xs.append(fold(r[:2 * n]) if square else None)
            new_ts.append(t + fold(r[-2 * n:]) if update else t)
        xs, ts = new_xs, new_ts
        if not square:
            return [t[:, :n] for t in ts]
        p *= 2


def _delta_kernel(blk_ref, seq_ref, first_ref, last_ref, q_ref, k_ref, v_ref, pc_ref, pr_ref, s0_ref, o_ref, sf_ref,
                  s_scr, *, hb, cg, reverse):
    n = pl.program_id(1)

    @pl.when(first_ref[n] == 1)
    def _():
        s_scr[...] = s0_ref[...]

    c = DN_CHUNK
    row = lax.broadcasted_iota(jnp.int32, (c, 2 * c), 0)
    col = lax.broadcasted_iota(jnp.int32, (c, 2 * c), 1)
    col = jnp.where(col < c, col, col - c)
    incl = (row <= col) if reverse else (row >= col)
    strict = (row < col) if reverse else (row > col)
    last = 0 if reverse else c - 1
    order = range(cg - 1, -1, -1) if reverse else range(cg)

    shared = {}
    for ci in order:
        rows = slice(ci * c, (ci + 1) * c)
        for kh in range(hb // 2):
            q = q_ref[rows, kh * LANES:(kh + 1) * LANES]
            k = k_ref[rows, kh * LANES:(kh + 1) * LANES]
            kb = k.astype(BF16)
            shared[ci, kh] = dict(q=q, k=k, kk=_dot_nt(kb, jnp.concatenate([kb, kb], axis=0)),
                                  qk=_dot_nt(q.astype(BF16), kb), kt=k.T)

    probs = [(ci, hh) for ci in order for hh in range(hb)]
    pre = {}
    for ci, hh in probs:
        pc = pc_ref[ci * c:(ci + 1) * c, :]
        pr = pr_ref[ci]
        beta = pc[:, hh:hh + 1]
        g_col = pc[:, hb + hh:hb + hh + 1]
        g_row2 = pr[hb + hh:hb + hh + 1, :]
        g_row = g_row2[:, :c]
        g_last = g_row[:, last:last + 1]
        decay2 = jnp.where(incl, jnp.exp(jnp.where(incl, g_col - g_row2, 0.0)), 0.0)
        pre[ci, hh] = dict(beta=beta, e_g=jnp.exp(g_col), decay=decay2[:, :c], gl=jnp.exp(g_last),
                           kscale=jnp.exp(g_last - g_row),
                           a=jnp.where(strict, beta * decay2 * shared[ci, hh // 2]["kk"], 0.0))
    tinvs = _unit_tri_inverses([pre[p]["a"] for p in probs])

    prep = {}
    for (ci, hh), tinv in zip(probs, tinvs):
        sh, pp = shared[ci, hh // 2], pre[ci, hh]
        v = v_ref[ci * c:(ci + 1) * c, hh * LANES:(hh + 1) * LANES]
        sol = _bdot(tinv, jnp.concatenate([pp["beta"] * v, (pp["beta"] * pp["e_g"]) * sh["k"]], axis=1))
        prep[ci, hh] = dict(
            u=sol[:, :LANES],
            wk_qg=jnp.concatenate([sol[:, LANES:], sh["q"] * pp["e_g"]], axis=0).astype(BF16),
            aqk=(pp["decay"] * sh["qk"]).astype(BF16),
            kdt=(sh["kt"] * pp["kscale"]).astype(BF16),
            gl=pp["gl"])

    states = [s_scr[hh] for hh in range(hb)]
    for ci in order:
        boths = [_dot(prep[ci, hh]["wk_qg"], states[hh].astype(BF16)) for hh in range(hb)]
        ws = [(prep[ci, hh]["u"] - boths[hh][:c]).astype(BF16) for hh in range(hb)]
        for hh in range(hb):
            o_ref[ci * c:(ci + 1) * c, hh * LANES:(hh + 1) * LANES] = (
                boths[hh][c:] + _dot(prep[ci, hh]["aqk"], ws[hh]))
        states = [prep[ci, hh]["gl"] * states[hh] + _dot(prep[ci, hh]["kdt"], ws[hh]) for hh in range(hb)]
    for hh in range(hb):
        s_scr[hh] = states[hh]

    @pl.when(last_ref[n] == 1)
    def _():
        sf_ref[...] = s_scr[...]


def _delta_scan(qkvc, p_col, p_row, s0, seq_lens, *, hk, hv, dk, dv, direction, hb, cg):
    assert dk == LANES and dv == LANES and hv == 2 * hk
    c = DN_CHUNK * cg
    hg = hv // hb
    reverse = direction == 1
    blk, seq, first, last = [], [], [], []
    start = 0
    for s, length in enumerate(seq_lens):
        nb = length // c
        ids = list(range(start, start + nb))
        blk += ids[::-1] if reverse else ids
        seq += [s] * nb
        first += [1] + [0] * (nb - 1)
        last += [0] * (nb - 1) + [1]
        start += nb
    tables = [jnp.asarray(np.asarray(a, np.int32)) for a in (blk, seq, first, last)]
    t = start * c

    qw = (hb // 2) * dk
    vw = hb * dv
    state_spec = pl.BlockSpec((None, hb, dk, dv), lambda g, n, bt, st, ft, lt: (st[n], g, 0, 0))
    grid_spec = pltpu.PrefetchScalarGridSpec(
        num_scalar_prefetch=4, grid=(hg, len(blk)),
        in_specs=[pl.BlockSpec((c, qw), lambda g, n, bt, st, ft, lt: (bt[n], g)),
                  pl.BlockSpec((c, qw), lambda g, n, bt, st, ft, lt: (bt[n], hk * dk // qw + g)),
                  pl.BlockSpec((c, vw), lambda g, n, bt, st, ft, lt: (bt[n], 2 * hk * dk // vw + g)),
                  pl.BlockSpec((None, None, c, 2 * hb), lambda g, n, bt, st, ft, lt: (g, direction, bt[n], 0)),
                  pl.BlockSpec((None, None, cg, 2 * hb, 2 * DN_CHUNK),
                               lambda g, n, bt, st, ft, lt: (g, direction, bt[n], 0, 0)),
                  state_spec],
        out_specs=[pl.BlockSpec((c, vw), lambda g, n, bt, st, ft, lt: (bt[n], g)), state_spec],
        scratch_shapes=[pltpu.VMEM((hb, dk, dv), F32)])
    return pl.pallas_call(
        functools.partial(_delta_kernel, hb=hb, cg=cg, reverse=reverse),
        grid_spec=grid_spec,
        out_shape=[jax.ShapeDtypeStruct((t, hv * dv), F32), jax.ShapeDtypeStruct(s0.shape, F32)],
        compiler_params=_cparams("parallel", "arbitrary"), name="dn_scan",
    )(*tables, qkvc, qkvc, qkvc, p_col, p_row, s0)


def _gated_norm_kernel(of_ref, ob_ref, z_ref, g_ref, o_ref):
    for c in range(o_ref.shape[1] // LANES):
        sl = slice(c * LANES, (c + 1) * LANES)
        o = of_ref[:, sl] + ob_ref[:, sl]
        o = o * lax.rsqrt(jnp.mean(o * o, axis=-1, keepdims=True) + EPS) * g_ref[...]
        o_ref[:, sl] = (o * _silu(z_ref[:, sl])).astype(o_ref.dtype)


def _gated_norm(o_f, o_b, qkvz, z_col0, norm_g):
    t, w = o_f.shape
    tb = _pick(t, 256)
    cb = _pick(math.gcd(w, z_col0), 1024)
    tok = pl.BlockSpec((tb, cb), lambda i, j: (i, j))
    return pl.pallas_call(
        _gated_norm_kernel, grid=(t // tb, w // cb),
        in_specs=[tok, tok, pl.BlockSpec((tb, cb), lambda i, j: (i, z_col0 // cb + j)),
                  pl.BlockSpec((1, LANES), lambda i, j: (0, 0))],
        out_specs=tok, out_shape=jax.ShapeDtypeStruct((t, w), BF16),
        compiler_params=_cparams("parallel", "parallel"), name="dn_gated_norm",
    )(o_f, o_b, qkvz, norm_g.reshape(1, LANES).astype(F32))


def _na_ctx_kernel(q_ref, k_ref, v_ref, o_ref, *, scale):
    q = (q_ref[...] * scale).astype(BF16)
    s = _dot_nt(q, k_ref[...].astype(BF16))
    p = jnp.exp(s - jnp.max(s, axis=-1, keepdims=True))
    o = _dot(p.astype(BF16), v_ref[...].astype(BF16)) / jnp.sum(p, axis=-1, keepdims=True)
    o_ref[...] = o.astype(o_ref.dtype)


def _na_context(qkv, n_seq, seq_len, heads):
    blk = lambda off: pl.BlockSpec((seq_len, LANES), lambda b, h: (b, off * heads + h))
    return pl.pallas_call(
        functools.partial(_na_ctx_kernel, scale=LANES ** -0.5),
        grid=(n_seq, heads), in_specs=[blk(0), blk(1), blk(2)],
        out_specs=pl.BlockSpec((seq_len, LANES), lambda b, h: (b, h)),
        out_shape=jax.ShapeDtypeStruct((n_seq * seq_len, heads * LANES), BF16),
        compiler_params=_cparams("parallel", "parallel"), name="na_ctx",
    )(qkv, qkv, qkv)


def _na_lat_kernel(q_ref, k_ref, v_ref, kc_ref, vc_ref, bias_ref, o_ref, kb_scr, vb_scr, *, rows, wr, scale):
    kb_scr[...] = k_ref[...].astype(BF16)
    vb_scr[...] = v_ref[...].astype(BF16)
    kc = kc_ref[...].astype(BF16)
    vc = vc_ref[...].astype(BF16)
    w = GRID_W

    group = math.gcd(rows, NA_ROWS_PER_STEP)

    def body(it, carry):
        rs = [it * group + k for k in range(group)]
        r0s = [jnp.clip(r - wr // 2, 0, rows - wr) for r in rs]
        qs = [(q_ref[pl.ds(pl.multiple_of(r * w, w), w), :] * scale).astype(BF16) for r in rs]
        wins = [pl.ds(pl.multiple_of(r0 * w, w), wr * w) for r0 in r0s]
        s_lats = [_dot_nt(q, kb_scr[win, :]) for q, win in zip(qs, wins)]
        s_ctxs = [_dot_nt(q, kc) for q in qs]
        outs = []
        for r, r0, win, s_lat, s_ctx in zip(rs, r0s, wins, s_lats, s_ctxs):
            d0 = r0 - r + WIN_R - 1
            s_lat = s_lat + jnp.concatenate([bias_ref[d0 + 2 * p] for p in range(wr // 2)], axis=1)
            m = jnp.maximum(jnp.max(s_lat, axis=-1, keepdims=True), jnp.max(s_ctx, axis=-1, keepdims=True))
            p_lat = jnp.exp(s_lat - m)
            p_ctx = jnp.exp(s_ctx - m)
            den = jnp.sum(p_lat, axis=-1, keepdims=True) + jnp.sum(p_ctx, axis=-1, keepdims=True)
            outs.append((p_lat.astype(BF16), p_ctx.astype(BF16), den))
        for r, win, (p_lat, p_ctx, den) in zip(rs, wins, outs):
            o = _dot(p_lat, vb_scr[win, :]) + _dot(p_ctx, vc)
            o_ref[pl.ds(pl.multiple_of(r * w, w), w), :] = (o / den).astype(o_ref.dtype)
        return carry

    lax.fori_loop(0, rows // group, body, 0)


def _na_bias_pairs(rpb):
    col = np.arange(GRID_W)
    c0 = np.clip(col - WIN_C // 2, 0, GRID_W - WIN_C)
    col_ok = (col[None, :] >= c0[:, None]) & (col[None, :] < c0[:, None] + WIN_C)
    dc_idx = np.clip(col[None, :] - col[:, None], 1 - WIN_C, WIN_C - 1) + WIN_C - 1
    bias = jnp.where(col_ok[None, None], rpb.astype(F32)[:, :, dc_idx], NEG)
    return jnp.concatenate([bias[:, :-1], bias[:, 1:]], axis=-1)


def _na_latent(qkv, cache_k, cache_v, rpb, row0, n_seq, seq_len, heads):
    rows = seq_len // GRID_W
    wr = min(WIN_R, rows)
    assert wr % 2 == 0 and row0 % seq_len == 0
    b0 = row0 // seq_len
    past = cache_k.shape[1]
    bias = _na_bias_pairs(rpb)
    blk = lambda off: pl.BlockSpec((seq_len, LANES), lambda b, h: (b0 + b, off * heads + h))
    cache = pl.BlockSpec((None, past, LANES), lambda b, h: (b, 0, h))
    return pl.pallas_call(
        functools.partial(_na_lat_kernel, rows=rows, wr=wr, scale=LANES ** -0.5),
        grid=(n_seq, heads),
        in_specs=[blk(0), blk(1), blk(2), cache, cache,
                  pl.BlockSpec((None,) + bias.shape[1:], lambda b, h: (h, 0, 0, 0))],
        out_specs=pl.BlockSpec((seq_len, LANES), lambda b, h: (b, h)),
        out_shape=jax.ShapeDtypeStruct((n_seq * seq_len, heads * LANES), BF16),
        scratch_shapes=[pltpu.VMEM((seq_len, LANES), BF16), pltpu.VMEM((seq_len, LANES), BF16)],
        compiler_params=_cparams("parallel", "parallel"), name="na_lat",
    )(qkv, qkv, qkv, cache_k, cache_v, bias)


def _top16(s, row):
    n = s.shape[0]
    vals = []
    rank = jnp.full(s.shape, PEER_TOPK, jnp.int32)
    for r in range(PEER_TOPK):
        m = jnp.max(s, axis=0, keepdims=True)
        idx = jnp.min(jnp.where(s == m, row, n), axis=0, keepdims=True)
        hit = row == idx
        vals.append(m)
        rank = jnp.where(hit, r, rank)
        s = jnp.where(hit, -jnp.inf, s)
    return jnp.concatenate(vals, axis=0), rank


def _top16_distinct(s):
    vals = []
    rank = jnp.full(s.shape, PEER_TOPK, jnp.int32)
    for r in range(PEER_TOPK):
        m = jnp.max(s, axis=0, keepdims=True)
        hit = s == m
        vals.append(m)
        rank = jnp.where(hit, r, rank)
        s = jnp.where(hit, -jnp.inf, s)
    count = jnp.sum((rank < PEER_TOPK).astype(jnp.int32), axis=0, keepdims=True)
    return jnp.concatenate(vals, axis=0), rank, count


def _top16_pair(s1, s2, row):
    v1, rank1, c1 = _top16_distinct(s1)
    v2, rank2, c2 = _top16_distinct(s2)
    tied = jnp.max(jnp.maximum(jnp.abs(c1 - PEER_TOPK), jnp.abs(c2 - PEER_TOPK)))
    return lax.cond(tied > 0, lambda: _top16(s1, row) + _top16(s2, row), lambda: (v1, rank1, v2, rank2))


def _peer_topk_kernel(h_ref, wq_ref, keys_ref, n_ref, e1_ref, b_ref, e2_ref, q_scr, *, heads, half):
    tb = h_ref.shape[0]
    q_scr[...] = _dot(h_ref[...], wq_ref[...])
    q_ref = q_scr
    nk = keys_ref.shape[2]
    kk = PEER_TOPK
    assert kk == 2 * SUBLANES
    row = lax.broadcasted_iota(jnp.int32, (nk, tb), 0)
    row_k = lax.broadcasted_iota(jnp.int32, (kk, tb), 0)
    sub = lax.broadcasted_iota(jnp.int32, (SUBLANES, tb), 0)
    flat = jnp.concatenate([row_k] + [a * kk + sub for a in range(1, SUBLANES)]
                           + [(SUBLANES + g * SUBLANES + sub) * kk for g in range(kk // SUBLANES - 1)], axis=0)
    for h in range(heads):
        q1 = q_ref[:, (2 * h) * half:(2 * h + 1) * half]
        q2 = q_ref[:, (2 * h + 1) * half:(2 * h + 2) * half]
        s1 = _dot_nt(keys_ref[h, 0], q1, HI)
        s2 = _dot_nt(keys_ref[h, 1], q2, HI)
        v1, rank1, v2, rank2 = _top16_pair(s1, s2, row)
        cand = jnp.concatenate([v1[0:1, :] + v2] + [v1[a:a + 1, :] + v2[:SUBLANES] for a in range(1, SUBLANES)]
                               + [v1[SUBLANES:, :] + v2[0:1, :]], axis=0)
        cnt = jnp.zeros((kk, tb), jnp.int32)
        zsum = jnp.zeros((1, tb), F32)
        best0 = None
        for r in range(kk):
            m = jnp.max(cand, axis=0, keepdims=True)
            idx = jnp.min(jnp.where(cand == m, flat, kk * kk), axis=0, keepdims=True)
            best0 = m if r == 0 else best0
            zsum = zsum + jnp.exp(m - best0)
            cnt = cnt + (row_k == idx // kk).astype(jnp.int32)
            cand = jnp.where(flat == idx, -jnp.inf, cand)
        n_i = jnp.zeros((nk, tb), jnp.int32)
        for a in range(kk):
            n_i = jnp.where(rank1 == a, cnt[a:a + 1, :], n_i)
        n_ref[h] = n_i.astype(F32)
        e1_ref[h] = jnp.exp(s1 - v1[0:1, :]) / zsum
        b_ref[h] = pltpu.bitcast(rank2.astype(F32).astype(BF16), jnp.uint32)
        e2_ref[h] = pltpu.bitcast(jnp.exp(s2 - v2[0:1, :]).astype(BF16), jnp.uint32)


def _peer_topk(h, w_q, keys):
    t, d = h.shape
    heads, _, nk, half = keys.shape
    qw = w_q.shape[1]
    tb = _pick(t, 256)
    out = [jax.ShapeDtypeStruct((heads, nk, t), F32)] * 2 + [jax.ShapeDtypeStruct((heads, nk // 2, t), jnp.uint32)] * 2
    spec = pl.BlockSpec((heads, nk, tb), lambda i: (0, 0, i))
    pspec = pl.BlockSpec((heads, nk // 2, tb), lambda i: (0, 0, i))
    return pl.pallas_call(
        functools.partial(_peer_topk_kernel, heads=heads, half=half),
        grid=(t // tb,),
        in_specs=[pl.BlockSpec((tb, d), lambda i: (i, 0)),
                  pl.BlockSpec((d, qw), lambda i: (0, 0)),
                  pl.BlockSpec(keys.shape, lambda i: (0, 0, 0, 0))],
        out_specs=[spec, spec, pspec, pspec], out_shape=out,
        scratch_shapes=[pltpu.VMEM((tb, qw), F32)],
        compiler_params=_cparams("parallel"), name="peer_topk",
    )(h, w_q, keys)


def _gelu(x):
    return 0.5 * x * (1.0 + lax.erf(x * (2.0 ** -0.5)))


def _peer_dense_kernel(*refs, heads, ib, parts):
    xt_ref = refs[0]
    u_refs = refs[1:1 + parts]
    vt_refs = refs[1 + parts:1 + 2 * parts]
    n_ref, e1_ref, b_ref, e2_ref, o_ref, acc_scr, a_scr, w_scr, cnt_scr, e1_scr = refs[1 + 2 * parts:]
    j = pl.program_id(1)

    @pl.when(j == 0)
    def _():
        acc_scr[...] = jnp.zeros(acc_scr.shape, F32)

    nk = 2 * b_ref.shape[1]
    d, tb = xt_ref.shape
    n_slab = a_scr.shape[0]
    es = a_scr.shape[1]
    ipb = es // nk
    dp = d // parts

    def scores(s):
        a_scr[s] = sum(_dot(u_refs[p][s * es:(s + 1) * es, :], xt_ref[p * dp:(p + 1) * dp, :]) for p in range(parts))

    for h in range(heads):
        for ii in range(ib):
            cnt_scr[h, ii] = jnp.broadcast_to(n_ref[h, ii:ii + 1, :], (BF16_ROWS, tb)).astype(BF16)
            e1_scr[h, ii] = jnp.broadcast_to(e1_ref[h, ii:ii + 1, :], (BF16_ROWS, tb)).astype(BF16)

    def gate_tiles(s):
        rep = nk // BF16_ROWS
        for il in range(ipb):
            ii = s * ipb + il
            rows = slice(il * nk, (il + 1) * nk)
            for tc in range(tb // LANES):
                lanes = slice(tc * LANES, (tc + 1) * LANES)
                g = jnp.zeros((rep, BF16_ROWS, LANES), BF16)
                for h in range(heads):
                    cnt = cnt_scr[h, ii, :, lanes][None]
                    e1 = e1_scr[h, ii, :, lanes][None]
                    b = pltpu.bitcast(b_ref[h, :, lanes], BF16).reshape(g.shape)
                    e2 = pltpu.bitcast(e2_ref[h, :, lanes], BF16).reshape(g.shape)
                    g = g + jnp.where(b < cnt, e1 * e2, jnp.zeros((), BF16))
                w_scr[s, rows, lanes] = g.reshape(nk, LANES) * _gelu(a_scr[s, rows, lanes]).astype(BF16)

    scores(0)
    for s in range(n_slab):
        if s + 1 < n_slab:
            scores(s + 1)
        gate_tiles(s)
        for p in range(parts):
            acc_scr[p * dp:(p + 1) * dp, :] += _dot(vt_refs[p][:, s * es:(s + 1) * es], w_scr[s])

    @pl.when(j == pl.num_programs(1) - 1)
    def _():
        o_ref[...] = acc_scr[...].T


def _peer_dense(ht, u_tab, v_tab, sel):
    d, t = ht.shape
    ne = u_tab.shape[0]
    n_i, e1, b_j, e2 = sel
    heads, nk, _ = n_i.shape
    tb = _pick(t, 512)
    ib = SUBLANES
    eb = ib * nk
    n_slab = 2
    u_tab = u_tab.astype(BF16)
    vt_tab = v_tab.reshape(ne // eb, eb, d).transpose(0, 2, 1).astype(BF16)
    n_i = n_i.reshape(heads, nk // ib, ib, t)
    e1 = e1.reshape(heads, nk // ib, ib, t)
    row_spec = pl.BlockSpec((heads, None, ib, tb), lambda i, j: (0, j, 0, i))
    col_spec = pl.BlockSpec((heads, nk // 2, tb), lambda i, j: (0, 0, i))
    parts = TABLE_PARTS
    dp = d // parts
    u_specs = [pl.BlockSpec((eb, dp), functools.partial(lambda i, j, p: (j, p), p=p)) for p in range(parts)]
    vt_specs = [pl.BlockSpec((None, dp, eb), functools.partial(lambda i, j, p: (j, p, 0), p=p)) for p in range(parts)]
    return pl.pallas_call(
        functools.partial(_peer_dense_kernel, heads=heads, ib=ib, parts=parts),
        grid=(t // tb, ne // eb),
        in_specs=[pl.BlockSpec((d, tb), lambda i, j: (0, i))] + u_specs + vt_specs
        + [row_spec, row_spec, col_spec, col_spec],
        out_specs=pl.BlockSpec((tb, d), lambda i, j: (i, 0)),
        out_shape=jax.ShapeDtypeStruct((t, d), F32),
        scratch_shapes=[pltpu.VMEM((d, tb), F32), pltpu.VMEM((n_slab, eb // n_slab, tb), F32),
                        pltpu.VMEM((n_slab, eb // n_slab, tb), BF16),
                        pltpu.VMEM((heads, ib, BF16_ROWS, tb), BF16), pltpu.VMEM((heads, ib, BF16_ROWS, tb), BF16)],
        compiler_params=_cparams("parallel", "arbitrary"), name="peer_dense",
    )(ht, *([u_tab] * parts), *([vt_tab] * parts), n_i, e1, b_j, e2)


def _peer(h, ht, w_q, keys, u_tab, v_tab):
    sel = _peer_topk(h, w_q.astype(BF16), keys.astype(F32))
    return _peer_dense(ht, u_tab, v_tab, sel)


def _seq_flags(groups, n_ctx_seq, tb):
    ctx_len = groups.ctx_tokens // n_ctx_seq
    starts = np.concatenate([np.arange(n_ctx_seq) * ctx_len,
                             groups.ctx_tokens + np.arange(groups.n_lat) * groups.lat_len])
    ends = np.concatenate([starts[1:], [groups.total]])
    blk = np.arange(groups.total // tb) * tb
    has_prev = ~np.isin(blk, starts)
    has_next = ~np.isin(blk + tb, ends)
    return {"tb": tb, "has_prev": jnp.asarray(has_prev, jnp.int32), "has_next": jnp.asarray(has_next, jnp.int32)}


def _deltanet(h, x, gate, groups, n_ctx_seq, w_in, conv_w, a_log, dt_bias, norm_g, w_o, state_delta):
    hv = a_log.shape[-1]
    dk, dv = state_delta.shape[-2:]
    qkv_w = conv_w.shape[1]
    v_w = hv * dv
    hk = (qkv_w - v_w) // (2 * dk)
    ctx_len = groups.ctx_tokens // n_ctx_seq
    t = groups.total
    c = DN_CHUNK

    qkvz = _matmul(h, w_in.astype(BF16), n_cols=qkv_w + v_w)
    p = _gates(h, w_in[:, qkv_w + v_w:].astype(F32), a_log, dt_bias)
    tb = _pick(math.gcd(ctx_len, groups.lat_len), 256)
    qkvc = _short_conv(qkvz, conv_w.astype(F32), _seq_flags(groups, n_ctx_seq, tb), hk * dk, dk)

    hb = min(DN_HEADS_PER_STEP, hv)
    cg = DN_CHUNKS_PER_STEP
    hg = hv // hb
    p5 = p.reshape(t, 2, 2, hg, hb).transpose(3, 2, 0, 1, 4).reshape(hg, 2, t, 2 * hb)
    p_row = p5.reshape(hg, 2, t // c, c, 2 * hb).transpose(0, 1, 2, 4, 3)
    p_row = jnp.concatenate([p_row, p_row], axis=-1)
    seq_lens = [ctx_len] * n_ctx_seq + [groups.lat_len] * groups.n_lat
    zero_state = jnp.zeros((n_ctx_seq, hv, dk, dv), F32)
    outs, states = [], []
    for direction in range(2):
        s0 = jnp.concatenate([zero_state, state_delta[:, direction].astype(F32)], axis=0)
        o, s_fin = _delta_scan(qkvc, p5, p_row, s0, seq_lens, hk=hk, hv=hv, dk=dk, dv=dv, direction=direction,
                               hb=hb, cg=cg)
        outs.append(o)
        states.append(s_fin[:n_ctx_seq])
    og = _gated_norm(outs[0], outs[1], qkvz, qkv_w, norm_g)
    x = _matmul(og, w_o.astype(BF16), groups=groups, res=x, gate=gate)
    return x, jnp.stack(states, axis=1)


def _natten(h, x, gate, groups, n_ctx_seq, w_qkv, rpb, w_o, cache_k, cache_v):
    d = h.shape[1]
    heads = d // LANES
    ctx_len = groups.ctx_tokens // n_ctx_seq
    qkv = _matmul(h, w_qkv.astype(BF16))
    o_c = _na_context(qkv, n_ctx_seq, ctx_len, heads)
    n_lat, past = cache_k.shape[:2]
    o_l = _na_latent(qkv, cache_k.reshape(n_lat, past, d), cache_v.reshape(n_lat, past, d), rpb,
                     groups.ctx_tokens, groups.n_lat, groups.lat_len, heads)
    x = _matmul(jnp.concatenate([o_c, o_l], axis=0), w_o.astype(BF16), groups=groups, res=x, gate=gate)
    k_c = qkv[:groups.ctx_tokens, d:2 * d].reshape(n_ctx_seq, ctx_len, heads, LANES)
    v_c = qkv[:groups.ctx_tokens, 2 * d:].reshape(n_ctx_seq, ctx_len, heads, LANES)
    return x, k_c, v_c


def kernel(x_prompt, x_sample, c, state_delta, cache_k, cache_v, c_ctx, ada_w, ada_b, norm1_g, norm2_g, final_g,
           dn_w_in, dn_conv_w, dn_a_log, dn_dt_bias, dn_norm_g, dn_w_o, na_w_qkv, na_rpb, na_w_o,
           peer_w_q, peer_keys, peer_u, peer_v):
    n_ctx_seq, ctx_len, d = x_prompt.shape
    n_lat, lat_len, _ = x_sample.shape
    depth = ada_w.shape[0]
    groups = _Groups(n_ctx_seq * ctx_len, n_lat, lat_len)
    n_grp = 1 + n_lat

    x = jnp.concatenate([x_prompt.reshape(-1, d), x_sample.reshape(-1, d)], axis=0).astype(F32)
    pad = (-n_grp) % SUBLANES
    cvecs = jnp.concatenate([c_ctx[None], c, jnp.zeros((pad, d), c.dtype)], axis=0).astype(F32)
    mods = _adaln(cvecs, ada_w.astype(F32), ada_b.astype(F32))[:, :n_grp]
    mods = mods.reshape(depth, n_grp, 6, 1, d).transpose(0, 2, 1, 3, 4)

    states, ctx_k, ctx_v = [], [], []
    delta = gate = None
    for i in range(depth):
        sh1, sc1, g1, sh2, sc2, g2 = (mods[i, k] for k in range(6))
        j = i // 2
        x, h = _normmod(x, norm1_g[i].astype(F32), groups, delta, gate, sh1, sc1)
        if i % 2 == 0:
            x, s_fin = _deltanet(h, x, g1, groups, n_ctx_seq, dn_w_in[j], dn_conv_w[j], dn_a_log[j], dn_dt_bias[j],
                                 dn_norm_g[j], dn_w_o[j], state_delta[:, j])
            states.append(s_fin)
        else:
            x, k_c, v_c = _natten(h, x, g1, groups, n_ctx_seq, na_w_qkv[j], na_rpb[j], na_w_o[j],
                                  cache_k[:, j], cache_v[:, j])
            ctx_k.append(k_c)
            ctx_v.append(v_c)
        x, h, ht = _normmod(x, norm2_g[i].astype(F32), groups, None, None, sh2, sc2, transposed=True)
        delta, gate = _peer(h, ht, peer_w_q[i], peer_keys[i], peer_u[i], peer_v[i]), g2
    final = functools.partial(_normmod, x, final_g.astype(F32), groups, delta, gate, out_dtype=F32)
    y_prompt = final(rows=(0, groups.ctx_tokens)).reshape(x_prompt.shape)
    y_sample = final(rows=(groups.ctx_tokens, n_lat * lat_len)).reshape(x_sample.shape)
    return (y_prompt, y_sample, jnp.stack(states, axis=1), jnp.stack(ctx_k, axis=1), jnp.stack(ctx_v, axis=1))
```
